```python
import jax, jax.numpy as jnp
from jax import lax
import numpy as np

D_MODEL = 4096
BATCH = 1
SEQ = 8192
DEPTH = 4

N_Q_HEADS = 32
N_KV_HEADS = 4
HEAD_DIM = 64
Q_GROUP = N_Q_HEADS // N_KV_HEADS
WINDOW = 128
ROPE_THETA = 10000.0
ATTN_W = N_Q_HEADS * HEAD_DIM
KV_W = N_KV_HEADS * HEAD_DIM
HGRN_HEADS = 16
HGRN_DK = 128
HGRN_DV = 128
HGRN_FW = HGRN_HEADS * HGRN_DK
HGRN_IW = HGRN_HEADS * HGRN_DV
CHUNK = 64
D_FF = 8192
CONV_WIDTH = 3
NORM_EPS = 1e-5
MASK_VALUE = -1e30
MIN_FORGET = 1e-30
MAX_LB = 0.999
IN_COLS = ATTN_W + 2 * KV_W + 2 * HGRN_FW + 2 * HGRN_IW + 2 * D_MODEL

kernel_name = "hybrid_swa_sink_hgrn2_gated_merge_convffn"


def rms_norm(x, g):
    xf = x.astype(jnp.float32)
    y = xf * lax.rsqrt(jnp.mean(xf * xf, axis=-1, keepdims=True) + NORM_EPS)
    return (y * g.astype(jnp.float32)).astype(x.dtype)


def rope_tables(positions):
    inv_freq = ROPE_THETA ** (-jnp.arange(0, HEAD_DIM, 2, dtype=jnp.float32) / HEAD_DIM)
    ang = positions.astype(jnp.float32)[..., None] * inv_freq
    return jnp.cos(ang), jnp.sin(ang)


def apply_rope(t, cos, sin):
    tf = t.astype(jnp.float32)
    t1, t2 = jnp.split(tf, 2, axis=-1)
    c, s = cos[:, :, None, :], sin[:, :, None, :]
    return jnp.concatenate([t1 * c - t2 * s, t2 * c + t1 * s], axis=-1).astype(t.dtype)


def sliding_window_attention(q, k, v, sinks):
    B, S, _, hd = q.shape
    nb = S // WINDOW
    f32 = jnp.float32
    qb = q.reshape(B, nb, WINDOW, N_KV_HEADS, Q_GROUP, hd).astype(f32)

    def banded(t):
        prev = jnp.pad(t, ((0, 0), (WINDOW, 0), (0, 0), (0, 0)))[:, :S]
        return jnp.concatenate([prev.reshape(B, nb, WINDOW, N_KV_HEADS, hd),
                                t.reshape(B, nb, WINDOW, N_KV_HEADS, hd)], axis=2)

    kb = banded(k).astype(f32)
    vb = banded(v)
    logits = jnp.einsum('bnqkgd,bnskd->bnkgqs', qb, kb) * (HEAD_DIM ** -0.5)
    qi = jnp.arange(WINDOW)[:, None] + WINDOW
    sj = jnp.arange(2 * WINDOW)[None, :]
    dist = qi - sj
    band = (dist >= 0) & (dist < WINDOW)
    not_before_start = (jnp.arange(nb)[:, None, None] > 0) | (sj[None] >= WINDOW)
    mask = band[None] & not_before_start
    logits = jnp.where(mask[None, :, None, None], logits, MASK_VALUE)
    sink = jnp.broadcast_to(sinks.astype(f32).reshape(1, 1, N_KV_HEADS, Q_GROUP, 1, 1),
                            logits.shape[:-1] + (1,))
    probs = jax.nn.softmax(jnp.concatenate([logits, sink], axis=-1), axis=-1)[..., :-1]
    out = jnp.einsum('bnkgqs,bnskd->bnqkgd', probs.astype(v.dtype), vb)
    return out.reshape(B, S, N_Q_HEADS * hd)


def hgrn2_chunked(q, f_logit, i, lb):
    B, S, H, DK = q.shape
    DV = i.shape[-1]
    f32 = jnp.float32
    lb = lb.astype(f32).reshape(H, DK)
    f = lb + (1.0 - lb) * jax.nn.sigmoid(f_logit.astype(f32))
    log_f = jnp.log(jnp.maximum(f, MIN_FORGET))
    key = 1.0 - f
    qf = q.astype(f32) * (DK ** -0.5)
    vf = i.astype(f32)
    n = S // CHUNK

    def to_chunks(t):
        return t.reshape(B, n, CHUNK, H, t.shape[-1]).transpose(1, 0, 3, 2, 4)

    causal = jnp.tril(jnp.ones((CHUNK, CHUNK), dtype=bool))[:, :, None]

    def step(state, inp):
        qc, kc, vc, gc = inp
        b = jnp.cumsum(gc, axis=2)
        o_inter = jnp.einsum('bhtk,bhkv->bhtv', qc * jnp.exp(b), state)
        diff = b[:, :, :, None, :] - b[:, :, None, :, :]
        decay = jnp.where(causal, jnp.exp(jnp.where(causal, diff, 0.0)), 0.0)
        scores = jnp.einsum('bhtk,bhsk,bhtsk->bhts', qc, kc, decay)
        o_intra = jnp.einsum('bhts,bhsv->bhtv', scores, vc)
        b_last = b[:, :, -1:, :]
        new_state = (jnp.exp(b_last[:, :, 0, :])[..., None] * state
                     + jnp.einsum('bhsk,bhsv->bhkv', kc * jnp.exp(b_last - b), vc))
        return new_state, o_inter + o_intra

    s0 = jnp.zeros((B, H, DK, DV), f32)
    _, o = lax.scan(step, s0, (to_chunks(qf), to_chunks(key), to_chunks(vf), to_chunks(log_f)))
    return o.transpose(1, 0, 3, 2, 4).reshape(B, S, H, DV)


def causal_depthwise_conv(u, w, b):
    K = w.shape[0]
    S = u.shape[1]
    up = jnp.pad(u, ((0, 0), (K - 1, 0), (0, 0)))
    out = b
    for j in range(K):
        out = out + w[j] * up[:, j:j + S]
    return out


def setup_inputs(seed: int = 0) -> dict:
    key = jax.random.key(seed)
    ks = jax.random.split(key, 18)
    f32 = jnp.float32

    def nrm(k, shape, scale):
        return jax.random.normal(k, shape, f32) * scale

    x = nrm(ks[0], (BATCH, SEQ, D_MODEL), 1.0)
    positions = (jnp.arange(SEQ, dtype=jnp.int32)[None, :]
                 + jax.random.randint(ks[1], (BATCH, 1), 0, 1024, dtype=jnp.int32))
    return {
        "x": x,
        "positions": positions,
        "norm1_g": 1.0 + nrm(ks[2], (DEPTH, D_MODEL), 0.02),
        "w_in": nrm(ks[3], (DEPTH, D_MODEL, IN_COLS), D_MODEL ** -0.5),
        "attn_sinks": nrm(ks[4], (DEPTH, N_Q_HEADS), 1.0),
        "lb_param": nrm(ks[5], (DEPTH, HGRN_FW), 1.0),
        "hgrn_norm_g": 1.0 + nrm(ks[6], (DEPTH, HGRN_IW), 0.02),
        "p_attn": nrm(ks[7], (DEPTH, ATTN_W, D_MODEL), ATTN_W ** -0.5),
        "p_hgrn": nrm(ks[8], (DEPTH, HGRN_IW, D_MODEL), HGRN_IW ** -0.5),
        "w_out": nrm(ks[9], (DEPTH, D_MODEL, D_MODEL), D_MODEL ** -0.5),
        "norm2_g": 1.0 + nrm(ks[10], (DEPTH, D_MODEL), 0.02),
        "w_ffn_in": nrm(ks[11], (DEPTH, D_MODEL, 2 * D_FF), D_MODEL ** -0.5),
        "conv_w": nrm(ks[12], (DEPTH, CONV_WIDTH, D_FF), CONV_WIDTH ** -0.5),
        "conv_b": nrm(ks[13], (DEPTH, D_FF), 0.01),
        "w_down": nrm(ks[14], (DEPTH, D_FF, D_MODEL), D_FF ** -0.5),
        "final_norm_g": 1.0 + nrm(ks[15], (D_MODEL,), 0.02),
    }


def reference(x, positions, norm1_g, w_in, attn_sinks, lb_param, hgrn_norm_g, p_attn, p_hgrn,
              w_out, norm2_g, w_ffn_in, conv_w, conv_b, w_down, final_norm_g):
    B, S, _ = x.shape
    cos, sin = rope_tables(positions)
    lb_soft = jax.nn.softmax(lb_param.astype(jnp.float32), axis=0)
    lower_bounds = jnp.clip(jnp.cumsum(lb_soft, axis=0) - lb_soft[0:1], 0.0, MAX_LB)
    sizes = [ATTN_W, KV_W, KV_W, HGRN_FW, HGRN_FW, HGRN_IW, HGRN_IW, D_MODEL, D_MODEL]
    offsets = [int(o) for o in np.cumsum(sizes)[:-1]]

    for l in range(DEPTH):
        h = rms_norm(x, norm1_g[l])
        proj = h @ w_in[l]
        q, k, v, hq, hf, hi, hg, ga, gb = jnp.split(proj, offsets, axis=-1)

        q = apply_rope(q.reshape(B, S, N_Q_HEADS, HEAD_DIM), cos, sin)
        k = apply_rope(k.reshape(B, S, N_KV_HEADS, HEAD_DIM), cos, sin)
        v = v.reshape(B, S, N_KV_HEADS, HEAD_DIM)
        a_out = sliding_window_attention(q, k, v, attn_sinks[l])

        o = hgrn2_chunked(hq.reshape(B, S, HGRN_HEADS, HGRN_DK),
                          hf.reshape(B, S, HGRN_HEADS, HGRN_DK),
                          hi.reshape(B, S, HGRN_HEADS, HGRN_DV),
                          lower_bounds[l])
        o = o * lax.rsqrt(jnp.mean(o * o, axis=-1, keepdims=True) + NORM_EPS)
        o = o.reshape(B, S, HGRN_IW) * hgrn_norm_g[l].astype(jnp.float32)
        b_out = (o * jax.nn.silu(hg.astype(jnp.float32))).astype(x.dtype)

        merged = (jax.nn.sigmoid(ga) * (a_out @ p_attn[l])
                  + jax.nn.sigmoid(gb) * (b_out @ p_hgrn[l]))
        x = x + merged @ w_out[l]

        h = rms_norm(x, norm2_g[l])
        u, g = jnp.split(h @ w_ffn_in[l], 2, axis=-1)
        u = causal_depthwise_conv(u, conv_w[l], conv_b[l])
        x = x + (jax.nn.gelu(u, approximate=False) * g) @ w_down[l]

    return rms_norm(x, final_norm_g)
```

```python
import functools

import jax
import jax.numpy as jnp
from jax import lax
from jax.experimental import pallas as pl
from jax.experimental.pallas import tpu as pltpu

N_Q_HEADS = 32
N_KV_HEADS = 4
HEAD_DIM = 64
Q_GROUP = N_Q_HEADS // N_KV_HEADS
WINDOW = 128
ROPE_THETA = 10000.0
ATTN_W = N_Q_HEADS * HEAD_DIM
KV_W = N_KV_HEADS * HEAD_DIM
HGRN_HEADS = 16
HGRN_DK = 128
HGRN_DV = 128
HGRN_FW = HGRN_HEADS * HGRN_DK
HGRN_IW = HGRN_HEADS * HGRN_DV
CONV_WIDTH = 3
NORM_EPS = 1e-5
MASK_VALUE = -1e30
MIN_FORGET = 1e-30
MAX_LB = 0.999

COL_Q = 0
COL_K = COL_Q + ATTN_W
COL_V = COL_K + KV_W
COL_HQ = COL_V + KV_W
COL_HF = COL_HQ + HGRN_FW
COL_HI = COL_HF + HGRN_FW
COL_HG = COL_HI + HGRN_IW
COL_GA = COL_HG + HGRN_IW

LANES = 128
BF16_ROWS = 16
VMEM_LIMIT = 56 * 1024 * 1024
HGRN_CHUNK = 16
NORM_ROWS = 64

f32 = jnp.float32
bf16 = jnp.bfloat16


def _params(sem):
    return pltpu.CompilerParams(dimension_semantics=sem, vmem_limit_bytes=VMEM_LIMIT)


def _norm_rows_into(x_ref, g_ref, h_ref, n_rows, dst_off):
    step = min(NORM_ROWS, n_rows)

    def body(c, carry):
        r0 = pl.multiple_of(c * step, step)
        x = x_ref[pl.ds(r0, step), :]
        ms = jnp.mean(x * x, axis=-1, keepdims=True)
        y = x * lax.rsqrt(ms + NORM_EPS) * g_ref[...]
        h_ref[pl.ds(dst_off + r0, step), :] = y.astype(h_ref.dtype)
        return carry

    lax.fori_loop(0, n_rows // step, body, 0)


def _proj_body(x_ref, g_ref, w_ref, o_ref, h_ref):
    @pl.when(pl.program_id(1) == 0)
    def _():
        _norm_rows_into(x_ref, g_ref, h_ref, x_ref.shape[0], 0)

    o_ref[...] = jnp.dot(h_ref[...], w_ref[...], preferred_element_type=f32).astype(o_ref.dtype)


def _norm_proj(x, g, w, bm, bn):
    m, d = x.shape
    n = w.shape[1]
    return pl.pallas_call(
        _proj_body,
        grid=(m // bm, n // bn),
        in_specs=[
            pl.BlockSpec((bm, d), lambda i, j: (i, 0)),
            pl.BlockSpec((1, d), lambda i, j: (0, 0)),
            pl.BlockSpec((d, bn), lambda i, j: (0, j)),
        ],
        out_specs=pl.BlockSpec((bm, bn), lambda i, j: (i, j)),
        out_shape=jax.ShapeDtypeStruct((m, n), bf16),
        scratch_shapes=[pltpu.VMEM((bm, d), bf16)],
        compiler_params=_params(("parallel", "arbitrary")),
        name="norm_proj",
    )(x, g.reshape(1, d), w)


def _rope_body(pos_ref, inv_ref, cos_ref, sin_ref):
    ang = pos_ref[...].astype(f32) * inv_ref[...]
    lane = lax.broadcasted_iota(jnp.int32, ang.shape, 1)
    first_half = (lane % HEAD_DIM) < (HEAD_DIM // 2)
    cos_ref[...] = jnp.cos(ang)
    s = jnp.sin(ang)
    sin_ref[...] = jnp.where(first_half, -s, s)


def _rope_tables(positions, bs):
    s = positions.shape[-1]
    half = HEAD_DIM // 2
    inv_freq = ROPE_THETA ** (-jnp.arange(0, HEAD_DIM, 2, dtype=f32) / HEAD_DIM)
    inv_row = jnp.tile(inv_freq, LANES // half).reshape(1, LANES)
    return pl.pallas_call(
        _rope_body,
        grid=(s // bs,),
        in_specs=[pl.BlockSpec((bs, 1), lambda i: (i, 0)),
                  pl.BlockSpec((1, LANES), lambda i: (0, 0))],
        out_specs=[pl.BlockSpec((bs, LANES), lambda i: (i, 0))] * 2,
        out_shape=[jax.ShapeDtypeStruct((s, LANES), f32)] * 2,
        compiler_params=_params(("parallel",)),
        name="rope_tables",
    )(positions.reshape(s, 1), inv_row)


def _rope_apply(t, cos, sin_signed, first_half):
    half = HEAD_DIM // 2
    fwd = pltpu.roll(t, LANES - half, axis=1)
    bwd = pltpu.roll(t, half, axis=1)
    return t * cos + jnp.where(first_half, fwd, bwd) * sin_signed


def _dup_head(pair, odd, low_lanes):
    swapped = pltpu.roll(pair, HEAD_DIM, axis=1)
    take_own = low_lanes != odd
    return jnp.where(take_own, pair, swapped)


def _attn_body(sink_ref, q_ref, kc_ref, kp_ref, vc_ref, vp_ref, cc_ref, sc_ref, cp_ref, sp_ref,
               o_ref, qs_ref, lg_ref, p_ref, inv_ref):
    n = pl.program_id(0)
    w = WINDOW
    lane = lax.broadcasted_iota(jnp.int32, (w, LANES), 1)
    low = lane < HEAD_DIM
    first_half = (lane % HEAD_DIM) < (HEAD_DIM // 2)
    cos_c, sin_c = cc_ref[...], sc_ref[...]
    cos_p, sin_p = cp_ref[...], sp_ref[...]

    qi = lax.broadcasted_iota(jnp.int32, (w, 2 * w), 0)
    sj = lax.broadcasted_iota(jnp.int32, (w, 2 * w), 1)
    dist = qi + w - sj
    mask = (dist >= 0) & (dist < w) & ((n > 0) | (sj >= w))

    pairs = Q_GROUP // 2
    scale = HEAD_DIM ** -0.5
    for kg in range(N_KV_HEADS):
        c = kg // 2
        odd = (kg % 2) == 1
        kcur = _rope_apply(kc_ref[:, c * LANES:(c + 1) * LANES].astype(f32), cos_c, sin_c, first_half)
        kprev = _rope_apply(kp_ref[:, c * LANES:(c + 1) * LANES].astype(f32), cos_p, sin_p, first_half)
        kk = jnp.concatenate([_dup_head(kprev, odd, low), _dup_head(kcur, odd, low)], axis=0).astype(bf16)
        vcur = vc_ref[:, c * LANES:(c + 1) * LANES].astype(f32)
        vprev = vp_ref[:, c * LANES:(c + 1) * LANES].astype(f32)
        vv = jnp.concatenate([_dup_head(vprev, odd, low), _dup_head(vcur, odd, low)], axis=0)
        low2 = jnp.concatenate([low, low], axis=0)
        v_even = jnp.where(low2, vv, 0.0).astype(bf16)
        v_odd = jnp.where(low2, 0.0, vv).astype(bf16)

        for p in range(pairs):
            col = kg * Q_GROUP * HEAD_DIM + p * LANES
            qp = _rope_apply(q_ref[:, col:col + LANES].astype(f32), cos_c, sin_c, first_half) * scale
            qs_ref[p * w:(p + 1) * w, :] = jnp.where(low, qp, 0.0).astype(bf16)
            qs_ref[(pairs + p) * w:(pairs + p + 1) * w, :] = jnp.where(low, 0.0, qp).astype(bf16)

        lg_ref[...] = lax.dot_general(qs_ref[...], kk, (((1,), (1,)), ((), ())),
                                      preferred_element_type=f32)
        for e in range(2):
            for p in range(pairs):
                slab = e * pairs + p
                head = kg * Q_GROUP + 2 * p + e
                sink = sink_ref[head]
                lg = jnp.where(mask, lg_ref[slab * w:(slab + 1) * w, :], MASK_VALUE)
                m = jnp.maximum(jnp.max(lg, axis=-1, keepdims=True), sink)
                ex = jnp.exp(lg - m)
                den = jnp.sum(ex, axis=-1, keepdims=True) + jnp.exp(sink - m)
                p_ref[slab * w:(slab + 1) * w, :] = ex.astype(bf16)
                inv_ref[slab * w:(slab + 1) * w, :] = 1.0 / den
        half_rows = pairs * w
        o_even = jnp.dot(p_ref[0:half_rows, :], v_even, preferred_element_type=f32) * inv_ref[0:half_rows, :]
        o_odd = jnp.dot(p_ref[half_rows:, :], v_odd, preferred_element_type=f32) * inv_ref[half_rows:, :]
        o = (o_even + o_odd).astype(o_ref.dtype)
        for p in range(pairs):
            col = kg * Q_GROUP * HEAD_DIM + p * LANES
            o_ref[:, col:col + LANES] = o[p * w:(p + 1) * w, :]


def _attention(proj, cos, sin, sinks):
    s = proj.shape[0]
    w = WINDOW
    nb = s // w
    kcol, vcol = COL_K // KV_W, COL_V // KV_W
    prev = lambda n, sk: (jnp.maximum(n - 1, 0), 0)
    grid_spec = pltpu.PrefetchScalarGridSpec(
        num_scalar_prefetch=1,
        grid=(nb,),
        in_specs=[
            pl.BlockSpec((w, ATTN_W), lambda n, sk: (n, 0)),
            pl.BlockSpec((w, KV_W), lambda n, sk: (n, kcol)),
            pl.BlockSpec((w, KV_W), lambda n, sk: (jnp.maximum(n - 1, 0), kcol)),
            pl.BlockSpec((w, KV_W), lambda n, sk: (n, vcol)),
            pl.BlockSpec((w, KV_W), lambda n, sk: (jnp.maximum(n - 1, 0), vcol)),
            pl.BlockSpec((w, LANES), lambda n, sk: (n, 0)),
            pl.BlockSpec((w, LANES), lambda n, sk: (n, 0)),
            pl.BlockSpec((w, LANES), prev),
            pl.BlockSpec((w, LANES), prev),
        ],
        out_specs=pl.BlockSpec((w, ATTN_W), lambda n, sk: (n, 0)),
        scratch_shapes=[
            pltpu.VMEM((Q_GROUP * w, LANES), bf16),
            pltpu.VMEM((Q_GROUP * w, 2 * w), f32),
            pltpu.VMEM((Q_GROUP * w, 2 * w), bf16),
            pltpu.VMEM((Q_GROUP * w, 1), f32),
        ],
    )
    return pl.pallas_call(
        _attn_body,
        grid_spec=grid_spec,
        out_shape=jax.ShapeDtypeStruct((s, ATTN_W), bf16),
        compiler_params=_params(("parallel",)),
        name="swa_attention",
    )(sinks.astype(f32), proj, proj, proj, proj, proj, cos, sin, cos, sin)


def _cumsum_rows(g, row):
    b = g
    shift = 1
    while shift < HGRN_CHUNK:
        b = b + jnp.where(row >= shift, pltpu.roll(b, shift, axis=0), 0.0)
        shift *= 2
    return b


def _hgrn_body(q_ref, z_ref, i_ref, og_ref, lbp_ref, ng_ref, o_ref, st_ref, *, layer):
    c = HGRN_CHUNK

    @pl.when(pl.program_id(1) == 0)
    def _():
        st_ref[...] = jnp.zeros_like(st_ref)

    lbp = lbp_ref[...]
    e = jnp.exp(lbp - jnp.max(lbp, axis=0, keepdims=True))
    sm = e / jnp.sum(e, axis=0, keepdims=True)
    cum = sm[0:1, :]
    for l in range(1, layer + 1):
        cum = cum + sm[l:l + 1, :]
    lower = jnp.clip(cum - sm[0:1, :], 0.0, MAX_LB)
    ng = ng_ref[...]
    row = lax.broadcasted_iota(jnp.int32, (c, LANES), 0)
    qscale = HGRN_DK ** -0.5

    def chunk(ci, carry):
        r0 = pl.multiple_of(ci * c, c)
        q = q_ref[pl.ds(r0, c), :].astype(f32) * qscale
        z = z_ref[pl.ds(r0, c), :].astype(f32)
        v = i_ref[pl.ds(r0, c), :].astype(f32)
        f = lower + (1.0 - lower) * jax.nn.sigmoid(z)
        g = jnp.log(jnp.maximum(f, MIN_FORGET))
        k = 1.0 - f
        b = _cumsum_rows(g, row)
        b_last = b[c - 1:c, :]

        acc = jnp.zeros((c, LANES), f32)
        for s in range(c):
            d = jnp.minimum(b - b[s:s + 1, :], 0.0)
            a = jnp.sum(q * k[s:s + 1, :] * jnp.exp(d), axis=-1, keepdims=True)
            acc = acc + jnp.where(row >= s, a, 0.0) * v[s:s + 1, :]

        st = st_ref[...]
        qd = (q * jnp.exp(b)).astype(bf16)
        acc = acc + lax.dot_general(qd, st.astype(bf16), (((1,), (1,)), ((), ())),
                                    preferred_element_type=f32)
        kd = (k * jnp.exp(b_last - b)).astype(bf16)
        upd = lax.dot_general(v.astype(bf16), kd, (((0,), (0,)), ((), ())),
                              preferred_element_type=f32)
        st_ref[...] = st * jnp.exp(b_last) + upd

        o = acc * lax.rsqrt(jnp.mean(acc * acc, axis=-1, keepdims=True) + NORM_EPS) * ng
        og = og_ref[pl.ds(r0, c), :].astype(f32)
        o_ref[pl.ds(r0, c), :] = (o * (og * jax.nn.sigmoid(og))).astype(o_ref.dtype)
        return carry

    lax.fori_loop(0, q_ref.shape[0] // c, chunk, 0)


def _hgrn(proj, lb_param, norm_g, layer, bt):
    s = proj.shape[0]
    depth = lb_param.shape[0]
    cq, cf, ci, cg = (COL_HQ // LANES, COL_HF // LANES, COL_HI // LANES, COL_HG // LANES)
    return pl.pallas_call(
        functools.partial(_hgrn_body, layer=layer),
        grid=(HGRN_HEADS, s // bt),
        in_specs=[
            pl.BlockSpec((bt, LANES), lambda h, t: (t, cq + h)),
            pl.BlockSpec((bt, LANES), lambda h, t: (t, cf + h)),
            pl.BlockSpec((bt, LANES), lambda h, t: (t, ci + h)),
            pl.BlockSpec((bt, LANES), lambda h, t: (t, cg + h)),
            pl.BlockSpec((depth, LANES), lambda h, t: (0, h)),
            pl.BlockSpec((1, LANES), lambda h, t: (0, h)),
        ],
        out_specs=pl.BlockSpec((bt, LANES), lambda h, t: (t, h)),
        out_shape=jax.ShapeDtypeStruct((s, HGRN_IW), bf16),
        scratch_shapes=[pltpu.VMEM((HGRN_DV, HGRN_DK), f32)],
        compiler_params=_params(("parallel", "arbitrary")),
        name="hgrn2",
    )(proj, proj, proj, proj, lb_param.astype(f32), norm_g.reshape(1, -1).astype(f32))


def _merge_body(a_ref, b_ref, pa_ref, ph_ref, ga_ref, gb_ref, o_ref):
    ya = jnp.dot(a_ref[...], pa_ref[...], preferred_element_type=f32)
    yb = jnp.dot(b_ref[...], ph_ref[...], preferred_element_type=f32)
    ga = jax.nn.sigmoid(ga_ref[...].astype(f32))
    gb = jax.nn.sigmoid(gb_ref[...].astype(f32))
    o_ref[...] = (ga * ya + gb * yb).astype(o_ref.dtype)


def _merge(a, b, pa, ph, proj, bm, bn):
    m, ka = a.shape
    kb = b.shape[1]
    d = pa.shape[1]
    ca = COL_GA // bn
    cb = (COL_GA + d) // bn
    return pl.pallas_call(
        _merge_body,
        grid=(m // bm, d // bn),
        in_specs=[
            pl.BlockSpec((bm, ka), lambda i, j: (i, 0)),
            pl.BlockSpec((bm, kb), lambda i, j: (i, 0)),
            pl.BlockSpec((ka, bn), lambda i, j: (0, j)),
            pl.BlockSpec((kb, bn), lambda i, j: (0, j)),
            pl.BlockSpec((bm, bn), lambda i, j: (i, ca + j)),
            pl.BlockSpec((bm, bn), lambda i, j: (i, cb + j)),
        ],
        out_specs=pl.BlockSpec((bm, bn), lambda i, j: (i, j)),
        out_shape=jax.ShapeDtypeStruct((m, d), bf16),
        compiler_params=_params(("parallel", "arbitrary")),
        name="gated_merge",
    )(a, b, pa, ph, proj, proj)


def _resid_body(a_ref, w_ref, x_ref, o_ref):
    o_ref[...] = x_ref[...] + jnp.dot(a_ref[...], w_ref[...], preferred_element_type=f32)


def _resid_matmul(a, w, x, bm, bn):
    m, k = a.shape
    d = w.shape[1]
    return pl.pallas_call(
        _resid_body,
        grid=(m // bm, d // bn),
        in_specs=[
            pl.BlockSpec((bm, k), lambda i, j: (i, 0)),
            pl.BlockSpec((k, bn), lambda i, j: (0, j)),
            pl.BlockSpec((bm, bn), lambda i, j: (i, j)),
        ],
        out_specs=pl.BlockSpec((bm, bn), lambda i, j: (i, j)),
        out_shape=jax.ShapeDtypeStruct((m, d), f32),
        input_output_aliases={2: 0},
        compiler_params=_params(("parallel", "arbitrary")),
        name="resid_matmul",
    )(a, w, x)


def _ffn_in_body(x_ref, xh_ref, g_ref, wu_ref, wg_ref, cw_ref, cb_ref, o_ref, h_ref, u_ref):
    bm = x_ref.shape[0]
    halo = BF16_ROWS

    @pl.when(pl.program_id(1) == 0)
    def _():
        _norm_rows_into(x_ref, g_ref, h_ref, bm, halo)

        @pl.when(pl.program_id(0) == 0)
        def _():
            h_ref[0:halo, :] = jnp.zeros((halo, h_ref.shape[1]), h_ref.dtype)

        @pl.when(pl.program_id(0) > 0)
        def _():
            _norm_rows_into(xh_ref, g_ref, h_ref, halo, 0)

    u_ref[...] = jnp.dot(h_ref[...], wu_ref[...], preferred_element_type=f32)
    gate = jnp.dot(h_ref[halo:, :], wg_ref[...], preferred_element_type=f32)
    cw = cw_ref[...]
    u = cb_ref[...] + cw[CONV_WIDTH - 1:CONV_WIDTH, :] * u_ref[halo:, :]
    for j in range(CONV_WIDTH - 1):
        back = CONV_WIDTH - 1 - j
        u = u + cw[j:j + 1, :] * u_ref[pl.ds(halo - back, bm), :]
    gelu = 0.5 * u * (1.0 + lax.erf(u * (2.0 ** -0.5)))
    o_ref[...] = (gelu * gate).astype(o_ref.dtype)


def _ffn_in(x, g, w, conv_w, conv_b, bm, bn):
    m, d = x.shape
    ff = conv_w.shape[1]
    halo = BF16_ROWS
    nj = ff // bn
    return pl.pallas_call(
        _ffn_in_body,
        grid=(m // bm, nj),
        in_specs=[
            pl.BlockSpec((bm, d), lambda i, j: (i, 0)),
            pl.BlockSpec((halo, d), lambda i, j: (jnp.maximum(i * (bm // halo) - 1, 0), 0)),
            pl.BlockSpec((1, d), lambda i, j: (0, 0)),
            pl.BlockSpec((d, bn), lambda i, j: (0, j)),
            pl.BlockSpec((d, bn), lambda i, j: (0, nj + j)),
            pl.BlockSpec((CONV_WIDTH, bn), lambda i, j: (0, j)),
            pl.BlockSpec((1, bn), lambda i, j: (0, j)),
        ],
        out_specs=pl.BlockSpec((bm, bn), lambda i, j: (i, j)),
        out_shape=jax.ShapeDtypeStruct((m, ff), bf16),
        scratch_shapes=[pltpu.VMEM((bm + halo, d), bf16), pltpu.VMEM((bm + halo, bn), f32)],
        compiler_params=_params(("parallel", "arbitrary")),
        name="ffn_in",
    )(x, x, g.reshape(1, d), w, w, conv_w, conv_b.reshape(1, ff))


def _final_norm_body(x_ref, g_ref, o_ref):
    x = x_ref[...]
    ms = jnp.mean(x * x, axis=-1, keepdims=True)
    o_ref[...] = x * lax.rsqrt(ms + NORM_EPS) * g_ref[...]


def _final_norm(x, g, bm):
    m, d = x.shape
    return pl.pallas_call(
        _final_norm_body,
        grid=(m // bm,),
        in_specs=[pl.BlockSpec((bm, d), lambda i: (i, 0)), pl.BlockSpec((1, d), lambda i: (0, 0))],
        out_specs=pl.BlockSpec((bm, d), lambda i: (i, 0)),
        out_shape=jax.ShapeDtypeStruct((m, d), f32),
        compiler_params=_params(("parallel",)),
        name="final_norm",
    )(x, g.reshape(1, d))


def _tiles(seq):
    bm = min(512, seq)
    return dict(bm=bm, bn=512, bt=min(1024, seq), rope_rows=min(512, seq), norm_rows=min(256, seq))


def kernel(x, positions, norm1_g, w_in, attn_sinks, lb_param, hgrn_norm_g, p_attn, p_hgrn, w_out,
           norm2_g, w_ffn_in, conv_w, conv_b, w_down, final_norm_g):
    batch, seq, d = x.shape
    depth = w_in.shape[0]
    t = _tiles(seq)
    bm, bn = t["bm"], t["bn"]
    outs = []
    for bi in range(batch):
        xs = x[bi]
        cos, sin = _rope_tables(positions[bi], t["rope_rows"])
        for l in range(depth):
            proj = _norm_proj(xs, norm1_g[l], w_in[l].astype(bf16), bm, bn)
            a_out = _attention(proj, cos, sin, attn_sinks[l])
            b_out = _hgrn(proj, lb_param, hgrn_norm_g[l], l, t["bt"])
            merged = _merge(a_out, b_out, p_attn[l].astype(bf16), p_hgrn[l].astype(bf16), proj, bm, bn)
            xs = _resid_matmul(merged, w_out[l].astype(bf16), xs, bm, bn)
            act = _ffn_in(xs, norm2_g[l], w_ffn_in[l].astype(bf16), conv_w[l], conv_b[l], bm, bn)
            xs = _resid_matmul(act, w_down[l].astype(bf16), xs, bm, bn)
        outs.append(_final_norm(xs, final_norm_g, t["norm_rows"]))
    return jnp.stack(outs, axis=0)
```

```python
import functools

import jax
import jax.numpy as jnp
from jax import lax
from jax.experimental import pallas as pl
from jax.experimental.pallas import tpu as pltpu

N_Q_HEADS = 32
N_KV_HEADS = 4
HEAD_DIM = 64
Q_GROUP = N_Q_HEADS // N_KV_HEADS
WINDOW = 128
ROPE_THETA = 10000.0
ATTN_W = N_Q_HEADS * HEAD_DIM
KV_W = N_KV_HEADS * HEAD_DIM
HGRN_HEADS = 16
HGRN_DK = 128
HGRN_DV = 128
HGRN_FW = HGRN_HEADS * HGRN_DK
HGRN_IW = HGRN_HEADS * HGRN_DV
CONV_WIDTH = 3
NORM_EPS = 1e-5
MASK_VALUE = -1e30
MIN_FORGET = 1e-30
MAX_LB = 0.999

COL_Q = 0
COL_K = COL_Q + ATTN_W
COL_V = COL_K + KV_W
COL_HQ = COL_V + KV_W
COL_HF = COL_HQ + HGRN_FW
COL_HI = COL_HF + HGRN_FW
COL_HG = COL_HI + HGRN_IW
COL_GA = COL_HG + HGRN_IW

LANES = 128
BF16_ROWS = 16
VMEM_LIMIT = 56 * 1024 * 1024
HGRN_CHUNK = 16
HGRN_SLAB = 8
HGRN_OUT_SLAB = 16
NORM_ROWS = 64

f32 = jnp.float32
bf16 = jnp.bfloat16


def _params(sem):
    return pltpu.CompilerParams(dimension_semantics=sem, vmem_limit_bytes=VMEM_LIMIT)


def _norm_rows_into(x_ref, g_ref, h_ref, n_rows, dst_off):
    step = min(NORM_ROWS, n_rows)

    def body(c, carry):
        r0 = pl.multiple_of(c * step, step)
        x = x_ref[pl.ds(r0, step), :]
        ms = jnp.mean(x * x, axis=-1, keepdims=True)
        y = x * lax.rsqrt(ms + NORM_EPS) * g_ref[...]
        h_ref[pl.ds(dst_off + r0, step), :] = y.astype(h_ref.dtype)
        return carry

    lax.fori_loop(0, n_rows // step, body, 0)


def _proj_body(x_ref, g_ref, w_ref, o_ref, h_ref):
    @pl.when(pl.program_id(1) == 0)
    def _():
        _norm_rows_into(x_ref, g_ref, h_ref, x_ref.shape[0], 0)

    o_ref[...] = jnp.dot(h_ref[...], w_ref[...], preferred_element_type=f32).astype(o_ref.dtype)


def _norm_proj(x, g, w, layer, bm, bn):
    m, d = x.shape
    n = w.shape[2]
    return pl.pallas_call(
        _proj_body,
        grid=(m // bm, n // bn),
        in_specs=[
            pl.BlockSpec((bm, d), lambda i, j: (i, 0)),
            pl.BlockSpec((1, d), lambda i, j: (0, 0)),
            pl.BlockSpec((None, d, bn), lambda i, j: (layer, 0, j)),
        ],
        out_specs=pl.BlockSpec((bm, bn), lambda i, j: (i, j)),
        out_shape=jax.ShapeDtypeStruct((m, n), bf16),
        scratch_shapes=[pltpu.VMEM((bm, d), bf16)],
        compiler_params=_params(("parallel", "arbitrary")),
        name="norm_proj",
    )(x, g.reshape(1, d), w)


def _rope_body(pos_ref, inv_ref, cos_ref, sin_ref):
    ang = pos_ref[...].astype(f32) * inv_ref[...]
    lane = lax.broadcasted_iota(jnp.int32, ang.shape, 1)
    first_half = (lane % HEAD_DIM) < (HEAD_DIM // 2)
    cos_ref[...] = jnp.cos(ang)
    s = jnp.sin(ang)
    sin_ref[...] = jnp.where(first_half, -s, s)


def _rope_tables(positions, bs):
    s = positions.shape[-1]
    half = HEAD_DIM // 2
    inv_freq = ROPE_THETA ** (-jnp.arange(0, HEAD_DIM, 2, dtype=f32) / HEAD_DIM)
    inv_row = jnp.tile(inv_freq, LANES // half).reshape(1, LANES)
    return pl.pallas_call(
        _rope_body,
        grid=(s // bs,),
        in_specs=[pl.BlockSpec((bs, 1), lambda i: (i, 0)),
                  pl.BlockSpec((1, LANES), lambda i: (0, 0))],
        out_specs=[pl.BlockSpec((bs, LANES), lambda i: (i, 0))] * 2,
        out_shape=[jax.ShapeDtypeStruct((s, LANES), f32)] * 2,
        compiler_params=_params(("parallel",)),
        name="rope_tables",
    )(positions.reshape(s, 1), inv_row)


def _rope_apply(t, cos, sin_signed, first_half):
    half = HEAD_DIM // 2
    fwd = pltpu.roll(t, LANES - half, axis=1)
    bwd = pltpu.roll(t, half, axis=1)
    return t * cos + jnp.where(first_half, fwd, bwd) * sin_signed


def _dup_head(pair, odd, low_lanes):
    swapped = pltpu.roll(pair, HEAD_DIM, axis=1)
    take_own = low_lanes != odd
    return jnp.where(take_own, pair, swapped)


def _attn_body(sink_ref, q_ref, kc_ref, kp_ref, vc_ref, vp_ref, cc_ref, sc_ref, cp_ref, sp_ref,
               o_ref, qs_ref, lg_ref, p_ref, inv_ref):
    n = pl.program_id(0)
    w = WINDOW
    lane = lax.broadcasted_iota(jnp.int32, (w, LANES), 1)
    low = lane < HEAD_DIM
    first_half = (lane % HEAD_DIM) < (HEAD_DIM // 2)
    cos_c, sin_c = cc_ref[...], sc_ref[...]
    cos_p, sin_p = cp_ref[...], sp_ref[...]

    qi = lax.broadcasted_iota(jnp.int32, (w, 2 * w), 0)
    sj = lax.broadcasted_iota(jnp.int32, (w, 2 * w), 1)
    dist = qi + w - sj
    mask = (dist >= 0) & (dist < w) & ((n > 0) | (sj >= w))

    pairs = Q_GROUP // 2
    scale = HEAD_DIM ** -0.5
    for kg in range(N_KV_HEADS):
        c = kg // 2
        odd = (kg % 2) == 1
        kcur = _rope_apply(kc_ref[:, c * LANES:(c + 1) * LANES].astype(f32), cos_c, sin_c, first_half)
        kprev = _rope_apply(kp_ref[:, c * LANES:(c + 1) * LANES].astype(f32), cos_p, sin_p, first_half)
        kk = jnp.concatenate([_dup_head(kprev, odd, low), _dup_head(kcur, odd, low)], axis=0).astype(bf16)
        vcur = vc_ref[:, c * LANES:(c + 1) * LANES].astype(f32)
        vprev = vp_ref[:, c * LANES:(c + 1) * LANES].astype(f32)
        vv = jnp.concatenate([_dup_head(vprev, odd, low), _dup_head(vcur, odd, low)], axis=0)
        low2 = jnp.concatenate([low, low], axis=0)
        v_even = jnp.where(low2, vv, 0.0).astype(bf16)
        v_odd = jnp.where(low2, 0.0, vv).astype(bf16)

        for p in range(pairs):
            col = kg * Q_GROUP * HEAD_DIM + p * LANES
            qp = _rope_apply(q_ref[:, col:col + LANES].astype(f32), cos_c, sin_c, first_half) * scale
            qs_ref[p * w:(p + 1) * w, :] = jnp.where(low, qp, 0.0).astype(bf16)
            qs_ref[(pairs + p) * w:(pairs + p + 1) * w, :] = jnp.where(low, 0.0, qp).astype(bf16)

        lg_ref[...] = lax.dot_general(qs_ref[...], kk, (((1,), (1,)), ((), ())),
                                      preferred_element_type=f32)
        for e in range(2):
            for p in range(pairs):
                slab = e * pairs + p
                head = kg * Q_GROUP + 2 * p + e
                sink = sink_ref[head]
                lg = jnp.where(mask, lg_ref[slab * w:(slab + 1) * w, :], MASK_VALUE)
                m = jnp.maximum(jnp.max(lg, axis=-1, keepdims=True), sink)
                ex = jnp.exp(lg - m)
                den = jnp.sum(ex, axis=-1, keepdims=True) + jnp.exp(sink - m)
                p_ref[slab * w:(slab + 1) * w, :] = ex.astype(bf16)
                inv_ref[slab * w:(slab + 1) * w, :] = 1.0 / den
        half_rows = pairs * w
        o_even = jnp.dot(p_ref[0:half_rows, :], v_even, preferred_element_type=f32) * inv_ref[0:half_rows, :]
        o_odd = jnp.dot(p_ref[half_rows:, :], v_odd, preferred_element_type=f32) * inv_ref[half_rows:, :]
        o = (o_even + o_odd).astype(o_ref.dtype)
        for p in range(pairs):
            col = kg * Q_GROUP * HEAD_DIM + p * LANES
            o_ref[:, col:col + LANES] = o[p * w:(p + 1) * w, :]


def _attention(proj, cos, sin, sinks):
    s = proj.shape[0]
    w = WINDOW
    nb = s // w
    kcol, vcol = COL_K // KV_W, COL_V // KV_W
    prev = lambda n, sk: (jnp.maximum(n - 1, 0), 0)
    grid_spec = pltpu.PrefetchScalarGridSpec(
        num_scalar_prefetch=1,
        grid=(nb,),
        in_specs=[
            pl.BlockSpec((w, ATTN_W), lambda n, sk: (n, 0)),
            pl.BlockSpec((w, KV_W), lambda n, sk: (n, kcol)),
            pl.BlockSpec((w, KV_W), lambda n, sk: (jnp.maximum(n - 1, 0), kcol)),
            pl.BlockSpec((w, KV_W), lambda n, sk: (n, vcol)),
            pl.BlockSpec((w, KV_W), lambda n, sk: (jnp.maximum(n - 1, 0), vcol)),
            pl.BlockSpec((w, LANES), lambda n, sk: (n, 0)),
            pl.BlockSpec((w, LANES), lambda n, sk: (n, 0)),
            pl.BlockSpec((w, LANES), prev),
            pl.BlockSpec((w, LANES), prev),
        ],
        out_specs=pl.BlockSpec((w, ATTN_W), lambda n, sk: (n, 0)),
        scratch_shapes=[
            pltpu.VMEM((Q_GROUP * w, LANES), bf16),
            pltpu.VMEM((Q_GROUP * w, 2 * w), f32),
            pltpu.VMEM((Q_GROUP * w, 2 * w), bf16),
            pltpu.VMEM((Q_GROUP * w, 1), f32),
        ],
    )
    return pl.pallas_call(
        _attn_body,
        grid_spec=grid_spec,
        out_shape=jax.ShapeDtypeStruct((s, ATTN_W), bf16),
        compiler_params=_params(("parallel",)),
        name="swa_attention",
    )(sinks.astype(f32), proj, proj, proj, proj, proj, cos, sin, cos, sin)


def _cumsum_chunks(g, row_in_chunk):
    b = g
    shift = 1
    while shift < HGRN_CHUNK:
        b = b + jnp.where(row_in_chunk >= shift, pltpu.roll(b, shift, axis=0), 0.0)
        shift *= 2
    return b


def _hgrn_body(q_ref, z_ref, i_ref, og_ref, lbp_ref, ng_ref, o_ref,
               st_ref, qd_ref, stb_ref, upd_ref, dec_ref, sc_ref, *, layer):
    c = HGRN_CHUNK
    hc = c // 2
    ga = HGRN_SLAB
    bt = q_ref.shape[0]
    n_chunks = bt // c
    slab = ga * c

    @pl.when(pl.program_id(1) == 0)
    def _():
        st_ref[...] = jnp.zeros_like(st_ref)

    lbp = lbp_ref[...]
    e = jnp.exp(lbp - jnp.max(lbp, axis=0, keepdims=True))
    sm = e / jnp.sum(e, axis=0, keepdims=True)
    cum = sm[0:1, :]
    for l in range(1, layer + 1):
        cum = cum + sm[l:l + 1, :]
    lower = jnp.clip(cum - sm[0:1, :], 0.0, MAX_LB)
    qscale = HGRN_DK ** -0.5
    row_in_chunk = lax.broadcasted_iota(jnp.int32, (slab, LANES), 0) % c
    upper = row_in_chunk >= hc
    half_id = lax.broadcasted_iota(jnp.int32, (2 * ga, hc, LANES), 0) % 2
    col = lax.broadcasted_iota(jnp.int32, (2 * ga, hc, LANES), 2) - hc * half_id
    tr = lax.broadcasted_iota(jnp.int32, (ga, c, c), 1)
    ts = lax.broadcasted_iota(jnp.int32, (ga, c, c), 2)
    causal = tr >= ts

    def phase_a(si, carry):
        r0 = pl.multiple_of(si * slab, slab)
        rows = pl.ds(r0, slab)
        q = q_ref[rows, :].astype(f32) * qscale
        z = z_ref[rows, :].astype(f32)
        f = lower + (1.0 - lower) * jax.nn.sigmoid(z)
        g = jnp.log(jnp.maximum(f, MIN_FORGET))
        k = 1.0 - f
        b = _cumsum_chunks(g, row_in_chunk)
        bc = b.reshape(ga, c, LANES)
        b_last = bc[:, c - 1:c, :]
        b_mid = bc[:, hc - 1:hc, :]
        b_last_r = jnp.broadcast_to(b_last, (ga, c, LANES)).reshape(slab, LANES)
        b_mid_r = jnp.broadcast_to(b_mid, (ga, c, LANES)).reshape(slab, LANES)
        vb = i_ref[rows, :]

        qd_ref[rows, :] = (q * jnp.exp(b)).astype(bf16)
        kd = (k * jnp.exp(b_last_r - b)).astype(bf16)
        qt = jnp.where(upper, q * jnp.exp(b - b_mid_r), 0.0).astype(bf16).reshape(ga, c, LANES)
        kt = jnp.where(upper, 0.0, k * jnp.exp(b_mid_r - b)).astype(bf16).reshape(ga, c, LANES)
        a_off = jnp.einsum("gtk,gsk->gts", qt, kt, preferred_element_type=f32)

        q3 = q.reshape(2 * ga, hc, LANES)
        k3 = k.reshape(2 * ga, hc, LANES)
        b3 = b.reshape(2 * ga, hc, LANES)
        a_diag = jnp.zeros((2 * ga, hc, LANES), f32)
        for s in range(hc):
            w = jnp.exp(b3 - b3[:, s:s + 1, :])
            a = jnp.sum(q3 * k3[:, s:s + 1, :] * w, axis=-1, keepdims=True)
            a_diag = jnp.where(col == s, a, a_diag)
        scores = a_diag.reshape(ga, c, LANES)[:, :, 0:c] + a_off
        sc_ref[rows, :] = jnp.where(causal, scores, 0.0).astype(bf16).reshape(slab, c)

        for gi in range(ga):
            upd_ref[si * ga + gi] = lax.dot_general(
                vb[gi * c:(gi + 1) * c, :], kd[gi * c:(gi + 1) * c, :],
                (((0,), (0,)), ((), ())), preferred_element_type=f32)
        dec_ref[pl.ds(si * ga, ga), :, :] = jnp.exp(b_last)
        return carry

    lax.fori_loop(0, bt // slab, phase_a, 0)

    def phase_b(ci, st):
        stb_ref[ci] = st.astype(bf16)
        return st * dec_ref[ci] + upd_ref[ci]

    st_ref[...] = lax.fori_loop(0, n_chunks, phase_b, st_ref[...], unroll=4)

    ng = ng_ref[...]
    gc = HGRN_OUT_SLAB
    oslab = gc * c

    def phase_c(si, carry):
        r0 = pl.multiple_of(si * oslab, oslab)
        rows = pl.ds(r0, oslab)
        outs = []
        for gi in range(gc):
            cr = pl.ds(r0 + gi * c, c)
            inter = lax.dot_general(qd_ref[cr, :], stb_ref[si * gc + gi],
                                    (((1,), (1,)), ((), ())), preferred_element_type=f32)
            outs.append(inter + jnp.dot(sc_ref[cr, :], i_ref[cr, :], preferred_element_type=f32))
        o = jnp.concatenate(outs, axis=0)
        o = o * lax.rsqrt(jnp.mean(o * o, axis=-1, keepdims=True) + NORM_EPS) * ng
        og = og_ref[rows, :].astype(f32)
        o_ref[rows, :] = (o * (og * jax.nn.sigmoid(og))).astype(o_ref.dtype)
        return carry

    lax.fori_loop(0, bt // oslab, phase_c, 0)


def _hgrn(proj, lb_param, norm_g, layer, bt):
    s = proj.shape[0]
    depth = lb_param.shape[0]
    n_chunks = bt // HGRN_CHUNK
    cq, cf, ci, cg = (COL_HQ // LANES, COL_HF // LANES, COL_HI // LANES, COL_HG // LANES)
    return pl.pallas_call(
        functools.partial(_hgrn_body, layer=layer),
        grid=(HGRN_HEADS, s // bt),
        in_specs=[
            pl.BlockSpec((bt, LANES), lambda h, t: (t, cq + h)),
            pl.BlockSpec((bt, LANES), lambda h, t: (t, cf + h)),
            pl.BlockSpec((bt, LANES), lambda h, t: (t, ci + h)),
            pl.BlockSpec((bt, LANES), lambda h, t: (t, cg + h)),
            pl.BlockSpec((depth, LANES), lambda h, t: (0, h)),
            pl.BlockSpec((1, LANES), lambda h, t: (0, h)),
        ],
        out_specs=pl.BlockSpec((bt, LANES), lambda h, t: (t, h)),
        out_shape=jax.ShapeDtypeStruct((s, HGRN_IW), bf16),
        scratch_shapes=[
            pltpu.VMEM((HGRN_DV, HGRN_DK), f32),
            pltpu.VMEM((bt, LANES), bf16),
            pltpu.VMEM((n_chunks, HGRN_DV, HGRN_DK), bf16),
            pltpu.VMEM((n_chunks, HGRN_DV, HGRN_DK), f32),
            pltpu.VMEM((n_chunks, 1, LANES), f32),
            pltpu.VMEM((bt, HGRN_CHUNK), bf16),
        ],
        compiler_params=_params(("parallel", "arbitrary")),
        name="hgrn2",
    )(proj, proj, proj, proj, lb_param.astype(f32), norm_g.reshape(1, -1).astype(f32))


def _merge_body(a_ref, b_ref, pa_ref, ph_ref, ga_ref, gb_ref, o_ref):
    ya = jnp.dot(a_ref[...], pa_ref[...], preferred_element_type=f32)
    yb = jnp.dot(b_ref[...], ph_ref[...], preferred_element_type=f32)
    ga = jax.nn.sigmoid(ga_ref[...].astype(f32))
    gb = jax.nn.sigmoid(gb_ref[...].astype(f32))
    o_ref[...] = (ga * ya + gb * yb).astype(o_ref.dtype)


def _merge(a, b, pa, ph, proj, layer, bm, bn):
    m, ka = a.shape
    kb = b.shape[1]
    d = pa.shape[2]
    ca = COL_GA // bn
    cb = (COL_GA + d) // bn
    return pl.pallas_call(
        _merge_body,
        grid=(m // bm, d // bn),
        in_specs=[
            pl.BlockSpec((bm, ka), lambda i, j: (i, 0)),
            pl.BlockSpec((bm, kb), lambda i, j: (i, 0)),
            pl.BlockSpec((None, ka, bn), lambda i, j: (layer, 0, j)),
            pl.BlockSpec((None, kb, bn), lambda i, j: (layer, 0, j)),
            pl.BlockSpec((bm, bn), lambda i, j: (i, ca + j)),
            pl.BlockSpec((bm, bn), lambda i, j: (i, cb + j)),
        ],
        out_specs=pl.BlockSpec((bm, bn), lambda i, j: (i, j)),
        out_shape=jax.ShapeDtypeStruct((m, d), bf16),
        compiler_params=_params(("parallel", "arbitrary")),
        name="gated_merge",
    )(a, b, pa, ph, proj, proj)


def _resid_body(a_ref, w_ref, x_ref, o_ref):
    o_ref[...] = x_ref[...] + jnp.dot(a_ref[...], w_ref[...], preferred_element_type=f32)


def _resid_matmul(a, w, x, layer, bm, bn):
    m, k = a.shape
    d = w.shape[2]
    return pl.pallas_call(
        _resid_body,
        grid=(m // bm, d // bn),
        in_specs=[
            pl.BlockSpec((bm, k), lambda i, j: (i, 0)),
            pl.BlockSpec((None, k, bn), lambda i, j: (layer, 0, j)),
            pl.BlockSpec((bm, bn), lambda i, j: (i, j)),
        ],
        out_specs=pl.BlockSpec((bm, bn), lambda i, j: (i, j)),
        out_shape=jax.ShapeDtypeStruct((m, d), f32),
        input_output_aliases={2: 0},
        compiler_params=_params(("parallel", "arbitrary")),
        name="resid_matmul",
    )(a, w, x)


def _ffn_in_body(x_ref, xh_ref, g_ref, wu_ref, wg_ref, cw_ref, cb_ref, o_ref, h_ref, u_ref):
    bm = x_ref.shape[0]
    halo = BF16_ROWS

    @pl.when(pl.program_id(1) == 0)
    def _():
        _norm_rows_into(x_ref, g_ref, h_ref, bm, halo)

        @pl.when(pl.program_id(0) == 0)
        def _():
            h_ref[0:halo, :] = jnp.zeros((halo, h_ref.shape[1]), h_ref.dtype)

        @pl.when(pl.program_id(0) > 0)
        def _():
            _norm_rows_into(xh_ref, g_ref, h_ref, halo, 0)

    u_ref[...] = jnp.dot(h_ref[...], wu_ref[...], preferred_element_type=f32)
    gate = jnp.dot(h_ref[halo:, :], wg_ref[...], preferred_element_type=f32)
    cw = cw_ref[...]
    u = cb_ref[...] + cw[CONV_WIDTH - 1:CONV_WIDTH, :] * u_ref[halo:, :]
    for j in range(CONV_WIDTH - 1):
        back = CONV_WIDTH - 1 - j
        u = u + cw[j:j + 1, :] * u_ref[pl.ds(halo - back, bm), :]
    gelu = 0.5 * u * (1.0 + lax.erf(u * (2.0 ** -0.5)))
    o_ref[...] = (gelu * gate).astype(o_ref.dtype)


def _ffn_in(x, g, w, conv_w, conv_b, layer, bm, bn):
    m, d = x.shape
    ff = conv_w.shape[2]
    halo = BF16_ROWS
    nj = ff // bn
    return pl.pallas_call(
        _ffn_in_body,
        grid=(m // bm, nj),
        in_specs=[
            pl.BlockSpec((bm, d), lambda i, j: (i, 0)),
            pl.BlockSpec((halo, d), lambda i, j: (jnp.maximum(i * (bm // halo) - 1, 0), 0)),
            pl.BlockSpec((1, d), lambda i, j: (0, 0)),
            pl.BlockSpec((None, d, bn), lambda i, j: (layer, 0, j)),
            pl.BlockSpec((None, d, bn), lambda i, j: (layer, 0, nj + j)),
            pl.BlockSpec((None, CONV_WIDTH, bn), lambda i, j: (layer, 0, j)),
            pl.BlockSpec((1, bn), lambda i, j: (0, j)),
        ],
        out_specs=pl.BlockSpec((bm, bn), lambda i, j: (i, j)),
        out_shape=jax.ShapeDtypeStruct((m, ff), bf16),
        scratch_shapes=[pltpu.VMEM((bm + halo, d), bf16), pltpu.VMEM((bm + halo, bn), f32)],
        compiler_params=_params(("parallel", "arbitrary")),
        name="ffn_in",
    )(x, x, g.reshape(1, d), w, w, conv_w, conv_b.reshape(1, ff))


def _final_norm_body(x_ref, g_ref, o_ref):
    x = x_ref[...]
    ms = jnp.mean(x * x, axis=-1, keepdims=True)
    o_ref[...] = x * lax.rsqrt(ms + NORM_EPS) * g_ref[...]


def _final_norm(x, g, bm):
    m, d = x.shape
    return pl.pallas_call(
        _final_norm_body,
        grid=(m // bm,),
        in_specs=[pl.BlockSpec((bm, d), lambda i: (i, 0)), pl.BlockSpec((1, d), lambda i: (0, 0))],
        out_specs=pl.BlockSpec((bm, d), lambda i: (i, 0)),
        out_shape=jax.ShapeDtypeStruct((m, d), f32),
        compiler_params=_params(("parallel",)),
        name="final_norm",
    )(x, g.reshape(1, d))


def _tiles(seq):
    bm = min(512, seq)
    return dict(bm=bm, bn=512, bt=min(1024, seq), rope_rows=min(512, seq), norm_rows=min(256, seq))


def kernel(x, positions, norm1_g, w_in, attn_sinks, lb_param, hgrn_norm_g, p_attn, p_hgrn, w_out,
           norm2_g, w_ffn_in, conv_w, conv_b, w_down, final_norm_g):
    batch, seq, d = x.shape
    depth = w_in.shape[0]
    t = _tiles(seq)
    bm, bn = t["bm"], t["bn"]
    w_in, p_attn, p_hgrn, w_out, w_ffn_in, w_down = (
        w.astype(bf16) for w in (w_in, p_attn, p_hgrn, w_out, w_ffn_in, w_down))
    outs = []
    for bi in range(batch):
        xs = x[bi]
        cos, sin = _rope_tables(positions[bi], t["rope_rows"])
        for l in range(depth):
            proj = _norm_proj(xs, norm1_g[l], w_in, l, bm, bn)
            a_out = _attention(proj, cos, sin, attn_sinks[l])
            b_out = _hgrn(proj, lb_param, hgrn_norm_g[l], l, t["bt"])
            merged = _merge(a_out, b_out, p_attn, p_hgrn, proj, l, bm, bn)
            xs = _resid_matmul(merged, w_out, xs, l, bm, bn)
            act = _ffn_in(xs, norm2_g[l], w_ffn_in, conv_w, conv_b[l], l, bm, bn)
            xs = _resid_matmul(act, w_down, xs, l, bm, bn)
        outs.append(_final_norm(xs, final_norm_g, t["norm_rows"]))
    return jnp.stack(outs, axis=0)
```

```python
import functools

import jax
import jax.numpy as jnp
from jax import lax
from jax.experimental import pallas as pl
from jax.experimental.pallas import tpu as pltpu

N_Q_HEADS = 32
N_KV_HEADS = 4
HEAD_DIM = 64
Q_GROUP = N_Q_HEADS // N_KV_HEADS
WINDOW = 128
ROPE_THETA = 10000.0
ATTN_W = N_Q_HEADS * HEAD_DIM
KV_W = N_KV_HEADS * HEAD_DIM
HGRN_HEADS = 16
HGRN_DK = 128
HGRN_DV = 128
HGRN_FW = HGRN_HEADS * HGRN_DK
HGRN_IW = HGRN_HEADS * HGRN_DV
CONV_WIDTH = 3
NORM_EPS = 1e-5
MASK_VALUE = -1e30
MIN_FORGET = 1e-30
MAX_LB = 0.999

COL_Q = 0
COL_K = COL_Q + ATTN_W
COL_V = COL_K + KV_W
COL_HQ = COL_V + KV_W
COL_HF = COL_HQ + HGRN_FW
COL_HI = COL_HF + HGRN_FW
COL_HG = COL_HI + HGRN_IW
COL_GA = COL_HG + HGRN_IW

LANES = 128
BF16_ROWS = 16
VMEM_LIMIT = 56 * 1024 * 1024
HGRN_CHUNK = 16
HGRN_SLAB = 8
HGRN_OUT_SLAB = 16
NORM_ROWS = 64

f32 = jnp.float32
bf16 = jnp.bfloat16


def _params(sem):
    return pltpu.CompilerParams(dimension_semantics=sem, vmem_limit_bytes=VMEM_LIMIT)


def _norm_rows_into(x_ref, g_ref, h_ref, n_rows, dst_off):
    step = min(NORM_ROWS, n_rows)

    def body(c, carry):
        r0 = pl.multiple_of(c * step, step)
        x = x_ref[pl.ds(r0, step), :]
        ms = jnp.mean(x * x, axis=-1, keepdims=True)
        y = x * lax.rsqrt(ms + NORM_EPS) * g_ref[...]
        h_ref[pl.ds(dst_off + r0, step), :] = y.astype(h_ref.dtype)
        return carry

    lax.fori_loop(0, n_rows // step, body, 0)


def _proj_body(x_ref, g_ref, w_ref, cos_ref, sin_ref, o_ref, h_ref, *, q_blocks, rope_blocks):
    j = pl.program_id(1)

    @pl.when(j == 0)
    def _():
        _norm_rows_into(x_ref, g_ref, h_ref, x_ref.shape[0], 0)

    w = w_ref[...].astype(bf16)
    acc = jnp.dot(h_ref[...], w, preferred_element_type=f32)

    @pl.when(j < rope_blocks)
    def _():
        half = HEAD_DIM // 2
        scale = jnp.where(j < q_blocks, HEAD_DIM ** -0.5, 1.0)
        cos = cos_ref[...] * scale
        sin = sin_ref[...] * scale
        first_half = (lax.broadcasted_iota(jnp.int32, cos.shape, 1) % HEAD_DIM) < half
        for c in range(acc.shape[1] // LANES):
            t = acc[:, c * LANES:(c + 1) * LANES]
            fwd = pltpu.roll(t, LANES - half, axis=1)
            bwd = pltpu.roll(t, half, axis=1)
            o_ref[:, c * LANES:(c + 1) * LANES] = (
                t * cos + jnp.where(first_half, fwd, bwd) * sin).astype(o_ref.dtype)

    @pl.when(j >= rope_blocks)
    def _():
        o_ref[...] = acc.astype(o_ref.dtype)


def _norm_proj(x, g, w, cos, sin, layer, bm, bn):
    m, d = x.shape
    n = w.shape[2]
    body = functools.partial(_proj_body, q_blocks=COL_K // bn, rope_blocks=COL_V // bn)
    return pl.pallas_call(
        body,
        grid=(m // bm, n // bn),
        in_specs=[
            pl.BlockSpec((bm, d), lambda i, j: (i, 0), pipeline_mode=pl.Buffered(1)),
            pl.BlockSpec((1, d), lambda i, j: (0, 0)),
            pl.BlockSpec((None, d, bn), lambda i, j: (layer, 0, j)),
            pl.BlockSpec((bm, LANES), lambda i, j: (i, 0)),
            pl.BlockSpec((bm, LANES), lambda i, j: (i, 0)),
        ],
        out_specs=pl.BlockSpec((bm, bn), lambda i, j: (i, j)),
        out_shape=jax.ShapeDtypeStruct((m, n), bf16),
        scratch_shapes=[pltpu.VMEM((bm, d), bf16)],
        compiler_params=_params(("parallel", "arbitrary")),
        name="norm_proj",
    )(x, g.reshape(1, d), w, cos, sin)


def _rope_body(pos_ref, inv_ref, cos_ref, sin_ref):
    ang = pos_ref[...].astype(f32) * inv_ref[...]
    lane = lax.broadcasted_iota(jnp.int32, ang.shape, 1)
    first_half = (lane % HEAD_DIM) < (HEAD_DIM // 2)
    cos_ref[...] = jnp.cos(ang)
    s = jnp.sin(ang)
    sin_ref[...] = jnp.where(first_half, -s, s)


def _rope_tables(positions, bs):
    s = positions.shape[-1]
    half = HEAD_DIM // 2
    inv_freq = ROPE_THETA ** (-jnp.arange(0, HEAD_DIM, 2, dtype=f32) / HEAD_DIM)
    inv_row = jnp.tile(inv_freq, LANES // half).reshape(1, LANES)
    return pl.pallas_call(
        _rope_body,
        grid=(s // bs,),
        in_specs=[pl.BlockSpec((bs, 1), lambda i: (i, 0)),
                  pl.BlockSpec((1, LANES), lambda i: (0, 0))],
        out_specs=[pl.BlockSpec((bs, LANES), lambda i: (i, 0))] * 2,
        out_shape=[jax.ShapeDtypeStruct((s, LANES), f32)] * 2,
        compiler_params=_params(("parallel",)),
        name="rope_tables",
    )(positions.reshape(s, 1), inv_row)


def _dup_head(pair, odd, low_lanes):
    swapped = pltpu.roll(pair, HEAD_DIM, axis=1)
    take_own = low_lanes != odd
    return jnp.where(take_own, pair, swapped)


def _attn_body(sink_ref, q_ref, kc_ref, kp_ref, vc_ref, vp_ref, o_ref, qs_ref, lg_ref, p_ref):
    n = pl.program_id(0)
    w = WINDOW
    pairs = Q_GROUP // 2
    low = lax.broadcasted_iota(jnp.int32, (w, LANES), 1) < HEAD_DIM
    low2 = lax.broadcasted_iota(jnp.int32, (2 * w, LANES), 1) < HEAD_DIM
    key_row = lax.broadcasted_iota(jnp.int32, (2 * w, LANES), 0)

    qi = lax.broadcasted_iota(jnp.int32, (w, 2 * w), 0)
    sj = lax.broadcasted_iota(jnp.int32, (w, 2 * w), 1)
    dist = qi + w - sj
    mask = (dist >= 0) & (dist < w) & ((n > 0) | (sj >= w))
    sink_col = lax.broadcasted_iota(jnp.int32, (1, 2 * w), 1) == 0

    for kg in range(N_KV_HEADS):
        c = kg // 2
        odd = (kg % 2) == 1
        cols = slice(c * LANES, (c + 1) * LANES)
        kband = jnp.concatenate([kp_ref[:, cols], kc_ref[:, cols]], axis=0).astype(f32)
        kk = _dup_head(kband, odd, low2).astype(bf16)
        vband = jnp.concatenate([vp_ref[:, cols], vc_ref[:, cols]], axis=0).astype(f32)
        vv = jnp.where(key_row == 0, 0.0, _dup_head(vband, odd, low2))
        ones = jnp.ones((2 * w, LANES), f32)
        v_even = jnp.concatenate([jnp.where(low2, vv, 0.0), jnp.where(low2, ones, 0.0)], axis=1).astype(bf16)
        v_odd = jnp.concatenate([jnp.where(low2, 0.0, vv), jnp.where(low2, 0.0, ones)], axis=1).astype(bf16)

        for p in range(pairs):
            col = kg * Q_GROUP * HEAD_DIM + p * LANES
            qp = q_ref[:, col:col + LANES]
            zero = jnp.zeros_like(qp)
            qs_ref[p * w:(p + 1) * w, :] = jnp.where(low, qp, zero)
            qs_ref[(pairs + p) * w:(pairs + p + 1) * w, :] = jnp.where(low, zero, qp)

        lg_ref[...] = lax.dot_general(qs_ref[...], kk, (((1,), (1,)), ((), ())),
                                      preferred_element_type=f32)
        for e in range(2):
            for p in range(pairs):
                slab = e * pairs + p
                rows = slice(slab * w, (slab + 1) * w)
                fill = jnp.where(sink_col, sink_ref[kg * Q_GROUP + 2 * p + e], MASK_VALUE)
                lg = jnp.where(mask, lg_ref[rows, :], fill)
                m = jnp.max(lg, axis=-1, keepdims=True)
                p_ref[rows, :] = jnp.exp(lg - m).astype(bf16)
        half_rows = pairs * w
        r = (jnp.dot(p_ref[0:half_rows, :], v_even, preferred_element_type=f32)
             + jnp.dot(p_ref[half_rows:, :], v_odd, preferred_element_type=f32))
        o = (r[:, 0:LANES] / r[:, LANES:]).astype(o_ref.dtype)
        for p in range(pairs):
            col = kg * Q_GROUP * HEAD_DIM + p * LANES
            o_ref[:, col:col + LANES] = o[p * w:(p + 1) * w, :]


def _attention(proj, sinks):
    s = proj.shape[0]
    w = WINDOW
    nb = s // w
    kcol, vcol = COL_K // KV_W, COL_V // KV_W
    grid_spec = pltpu.PrefetchScalarGridSpec(
        num_scalar_prefetch=1,
        grid=(nb,),
        in_specs=[
            pl.BlockSpec((w, ATTN_W), lambda n, sk: (n, 0)),
            pl.BlockSpec((w, KV_W), lambda n, sk: (n, kcol)),
            pl.BlockSpec((w, KV_W), lambda n, sk: (jnp.maximum(n - 1, 0), kcol)),
            pl.BlockSpec((w, KV_W), lambda n, sk: (n, vcol)),
            pl.BlockSpec((w, KV_W), lambda n, sk: (jnp.maximum(n - 1, 0), vcol)),
        ],
        out_specs=pl.BlockSpec((w, ATTN_W), lambda n, sk: (n, 0)),
        scratch_shapes=[
            pltpu.VMEM((Q_GROUP * w, LANES), bf16),
            pltpu.VMEM((Q_GROUP * w, 2 * w), f32),
            pltpu.VMEM((Q_GROUP * w, 2 * w), bf16),
        ],
    )
    return pl.pallas_call(
        _attn_body,
        grid_spec=grid_spec,
        out_shape=jax.ShapeDtypeStruct((s, ATTN_W), bf16),
        compiler_params=_params(("parallel",)),
        name="swa_attention",
    )(sinks.astype(f32), proj, proj, proj, proj, proj)


def _cumsum_chunks(g, row_in_chunk):
    b = g
    shift = 1
    while shift < HGRN_CHUNK:
        b = b + jnp.where(row_in_chunk >= shift, pltpu.roll(b, shift, axis=0), 0.0)
        shift *= 2
    return b


def _hgrn_body(q_ref, z_ref, i_ref, og_ref, lbp_ref, ng_ref, o_ref,
               st_ref, qd_ref, stb_ref, upd_ref, dec_ref, sc_ref, *, layer):
    c = HGRN_CHUNK
    hc = c // 2
    ga = HGRN_SLAB
    bt = q_ref.shape[0]
    n_chunks = bt // c
    slab = ga * c

    @pl.when(pl.program_id(1) == 0)
    def _():
        st_ref[...] = jnp.zeros_like(st_ref)

    lbp = lbp_ref[...]
    e = jnp.exp(lbp - jnp.max(lbp, axis=0, keepdims=True))
    sm = e / jnp.sum(e, axis=0, keepdims=True)
    cum = sm[0:1, :]
    for l in range(1, layer + 1):
        cum = cum + sm[l:l + 1, :]
    lower = jnp.clip(cum - sm[0:1, :], 0.0, MAX_LB)
    qscale = HGRN_DK ** -0.5
    row_in_chunk = lax.broadcasted_iota(jnp.int32, (slab, LANES), 0) % c
    upper = row_in_chunk >= hc
    half_id = lax.broadcasted_iota(jnp.int32, (2 * ga, hc, LANES), 0) % 2
    col = lax.broadcasted_iota(jnp.int32, (2 * ga, hc, LANES), 2) - hc * half_id
    tr = lax.broadcasted_iota(jnp.int32, (ga, c, c), 1)
    ts = lax.broadcasted_iota(jnp.int32, (ga, c, c), 2)
    causal = tr >= ts

    def phase_a(si, carry):
        r0 = pl.multiple_of(si * slab, slab)
        rows = pl.ds(r0, slab)
        q = q_ref[rows, :].astype(f32) * qscale
        z = z_ref[rows, :].astype(f32)
        f = lower + (1.0 - lower) * jax.nn.sigmoid(z)
        g = jnp.log(jnp.maximum(f, MIN_FORGET))
        k = 1.0 - f
        b = _cumsum_chunks(g, row_in_chunk)
        bc = b.reshape(ga, c, LANES)
        b_last = bc[:, c - 1:c, :]
        b_mid = bc[:, hc - 1:hc, :]
        b_last_r = jnp.broadcast_to(b_last, (ga, c, LANES)).reshape(slab, LANES)
        b_mid_r = jnp.broadcast_to(b_mid, (ga, c, LANES)).reshape(slab, LANES)
        vb = i_ref[rows, :]

        qd_ref[rows, :] = (q * jnp.exp(b)).astype(bf16)
        kd = (k * jnp.exp(b_last_r - b)).astype(bf16)
        qt = jnp.where(upper, q * jnp.exp(b - b_mid_r), 0.0).astype(bf16).reshape(ga, c, LANES)
        kt = jnp.where(upper, 0.0, k * jnp.exp(b_mid_r - b)).astype(bf16).reshape(ga, c, LANES)
        a_off = jnp.einsum("gtk,gsk->gts", qt, kt, preferred_element_type=f32)

        q3 = q.reshape(2 * ga, hc, LANES)
        k3 = k.reshape(2 * ga, hc, LANES)
        b3 = b.reshape(2 * ga, hc, LANES)
        a_diag = jnp.zeros((2 * ga, hc, LANES), f32)
        for s in range(hc):
            w = jnp.exp(b3 - b3[:, s:s + 1, :])
            a = jnp.sum(q3 * k3[:, s:s + 1, :] * w, axis=-1, keepdims=True)
            a_diag = jnp.where(col == s, a, a_diag)
        scores = a_diag.reshape(ga, c, LANES)[:, :, 0:c] + a_off
        sc_ref[rows, :] = jnp.where(causal, scores, 0.0).astype(bf16).reshape(slab, c)

        for gi in range(ga):
            upd_ref[si * ga + gi] = lax.dot_general(
                vb[gi * c:(gi + 1) * c, :], kd[gi * c:(gi + 1) * c, :],
                (((0,), (0,)), ((), ())), preferred_element_type=f32)
        dec_ref[pl.ds(si * ga, ga), :, :] = jnp.exp(b_last)
        return carry

    lax.fori_loop(0, bt // slab, phase_a, 0)

    def phase_b(ci, st):
        stb_ref[ci] = st.astype(bf16)
        return st * dec_ref[ci] + upd_ref[ci]

    st_ref[...] = lax.fori_loop(0, n_chunks, phase_b, st_ref[...], unroll=4)

    ng = ng_ref[...]
    gc = HGRN_OUT_SLAB
    oslab = gc * c

    def phase_c(si, carry):
        r0 = pl.multiple_of(si * oslab, oslab)
        rows = pl.ds(r0, oslab)
        outs = []
        for gi in range(gc):
            cr = pl.ds(r0 + gi * c, c)
            inter = lax.dot_general(qd_ref[cr, :], stb_ref[si * gc + gi],
                                    (((1,), (1,)), ((), ())), preferred_element_type=f32)
            outs.append(inter + jnp.dot(sc_ref[cr, :], i_ref[cr, :], preferred_element_type=f32))
        o = jnp.concatenate(outs, axis=0)
        o = o * lax.rsqrt(jnp.mean(o * o, axis=-1, keepdims=True) + NORM_EPS) * ng
        og = og_ref[rows, :].astype(f32)
        o_ref[rows, :] = (o * (og * jax.nn.sigmoid(og))).astype(o_ref.dtype)
        return carry

    lax.fori_loop(0, bt // oslab, phase_c, 0)


def _hgrn(proj, lb_param, norm_g, layer, bt):
    s = proj.shape[0]
    depth = lb_param.shape[0]
    n_chunks = bt // HGRN_CHUNK
    cq, cf, ci, cg = (COL_HQ // LANES, COL_HF // LANES, COL_HI // LANES, COL_HG // LANES)
    return pl.pallas_call(
        functools.partial(_hgrn_body, layer=layer),
        grid=(HGRN_HEADS, s // bt),
        in_specs=[
            pl.BlockSpec((bt, LANES), lambda h, t: (t, cq + h)),
            pl.BlockSpec((bt, LANES), lambda h, t: (t, cf + h)),
            pl.BlockSpec((bt, LANES), lambda h, t: (t, ci + h)),
            pl.BlockSpec((bt, LANES), lambda h, t: (t, cg + h)),
            pl.BlockSpec((depth, LANES), lambda h, t: (0, h)),
            pl.BlockSpec((1, LANES), lambda h, t: (0, h)),
        ],
        out_specs=pl.BlockSpec((bt, LANES), lambda h, t: (t, h)),
        out_shape=jax.ShapeDtypeStruct((s, HGRN_IW), bf16),
        scratch_shapes=[
            pltpu.VMEM((HGRN_DV, HGRN_DK), f32),
            pltpu.VMEM((bt, LANES), bf16),
            pltpu.VMEM((n_chunks, HGRN_DV, HGRN_DK), bf16),
            pltpu.VMEM((n_chunks, HGRN_DV, HGRN_DK), f32),
            pltpu.VMEM((n_chunks, 1, LANES), f32),
            pltpu.VMEM((bt, HGRN_CHUNK), bf16),
        ],
        compiler_params=_params(("parallel", "arbitrary")),
        name="hgrn2",
    )(proj, proj, proj, proj, lb_param.astype(f32), norm_g.reshape(1, -1).astype(f32))


def _merge_body(a_ref, b_ref, pa_ref, ph_ref, ga_ref, gb_ref, o_ref):
    ya = jnp.dot(a_ref[...], pa_ref[...], preferred_element_type=f32)
    yb = jnp.dot(b_ref[...], ph_ref[...], preferred_element_type=f32)
    ga = jax.nn.sigmoid(ga_ref[...].astype(f32))
    gb = jax.nn.sigmoid(gb_ref[...].astype(f32))
    o_ref[...] = (ga * ya + gb * yb).astype(o_ref.dtype)


def _merge(a, b, pa, ph, proj, layer, bm, bn):
    m, ka = a.shape
    kb = b.shape[1]
    d = pa.shape[2]
    ca = COL_GA // bn
    cb = (COL_GA + d) // bn
    return pl.pallas_call(
        _merge_body,
        grid=(m // bm, d // bn),
        in_specs=[
            pl.BlockSpec((bm, ka), lambda i, j: (i, 0)),
            pl.BlockSpec((bm, kb), lambda i, j: (i, 0)),
            pl.BlockSpec((None, ka, bn), lambda i, j: (layer, 0, j)),
            pl.BlockSpec((None, kb, bn), lambda i, j: (layer, 0, j)),
            pl.BlockSpec((bm, bn), lambda i, j: (i, ca + j)),
            pl.BlockSpec((bm, bn), lambda i, j: (i, cb + j)),
        ],
        out_specs=pl.BlockSpec((bm, bn), lambda i, j: (i, j)),
        out_shape=jax.ShapeDtypeStruct((m, d), bf16),
        compiler_params=_params(("parallel", "arbitrary")),
        name="gated_merge",
    )(a, b, pa, ph, proj, proj)


def _resid_body(a_ref, w_ref, x_ref, o_ref):
    o_ref[...] = x_ref[...] + jnp.dot(a_ref[...], w_ref[...], preferred_element_type=f32)


def _resid_matmul(a, w, x, layer, bm, bn):
    m, k = a.shape
    d = w.shape[2]
    return pl.pallas_call(
        _resid_body,
        grid=(m // bm, d // bn),
        in_specs=[
            pl.BlockSpec((bm, k), lambda i, j: (i, 0)),
            pl.BlockSpec((None, k, bn), lambda i, j: (layer, 0, j)),
            pl.BlockSpec((bm, bn), lambda i, j: (i, j)),
        ],
        out_specs=pl.BlockSpec((bm, bn), lambda i, j: (i, j)),
        out_shape=jax.ShapeDtypeStruct((m, d), f32),
        input_output_aliases={2: 0},
        compiler_params=_params(("parallel", "arbitrary")),
        name="resid_matmul",
    )(a, w, x)


def _ffn_in_body(x_ref, xh_ref, g_ref, wu_ref, wg_ref, cw_ref, cb_ref, o_ref, h_ref, u_ref):
    bm = x_ref.shape[0]
    halo = BF16_ROWS

    @pl.when(pl.program_id(1) == 0)
    def _():
        _norm_rows_into(x_ref, g_ref, h_ref, bm, halo)

        @pl.when(pl.program_id(0) == 0)
        def _():
            h_ref[0:halo, :] = jnp.zeros((halo, h_ref.shape[1]), h_ref.dtype)

        @pl.when(pl.program_id(0) > 0)
        def _():
            _norm_rows_into(xh_ref, g_ref, h_ref, halo, 0)

    u_ref[...] = jnp.dot(h_ref[...], wu_ref[...].astype(bf16), preferred_element_type=f32)
    gate = jnp.dot(h_ref[halo:, :], wg_ref[...].astype(bf16), preferred_element_type=f32)
    cw = cw_ref[...]
    u = cb_ref[...] + cw[CONV_WIDTH - 1:CONV_WIDTH, :] * u_ref[halo:, :]
    for j in range(CONV_WIDTH - 1):
        back = CONV_WIDTH - 1 - j
        u = u + cw[j:j + 1, :] * u_ref[pl.ds(halo - back, bm), :]
    gelu = 0.5 * u * (1.0 + lax.erf(u * (2.0 ** -0.5)))
    o_ref[...] = (gelu * gate).astype(o_ref.dtype)


def _ffn_in(x, g, w, conv_w, conv_b, layer, bm, bn):
    m, d = x.shape
    ff = conv_w.shape[2]
    halo = BF16_ROWS
    nj = ff // bn
    return pl.pallas_call(
        _ffn_in_body,
        grid=(m // bm, nj),
        in_specs=[
            pl.BlockSpec((bm, d), lambda i, j: (i, 0), pipeline_mode=pl.Buffered(1)),
            pl.BlockSpec((halo, d), lambda i, j: (jnp.maximum(i * (bm // halo) - 1, 0), 0)),
            pl.BlockSpec((1, d), lambda i, j: (0, 0)),
            pl.BlockSpec((None, d, bn), lambda i, j: (layer, 0, j)),
            pl.BlockSpec((None, d, bn), lambda i, j: (layer, 0, nj + j)),
            pl.BlockSpec((None, CONV_WIDTH, bn), lambda i, j: (layer, 0, j)),
            pl.BlockSpec((1, bn), lambda i, j: (0, j)),
        ],
        out_specs=pl.BlockSpec((bm, bn), lambda i, j: (i, j)),
        out_shape=jax.ShapeDtypeStruct((m, ff), bf16),
        scratch_shapes=[pltpu.VMEM((bm + halo, d), bf16), pltpu.VMEM((bm + halo, bn), f32)],
        compiler_params=_params(("parallel", "arbitrary")),
        name="ffn_in",
    )(x, x, g.reshape(1, d), w, w, conv_w, conv_b.reshape(1, ff))


def _final_norm_body(x_ref, g_ref, o_ref):
    x = x_ref[...]
    ms = jnp.mean(x * x, axis=-1, keepdims=True)
    o_ref[...] = x * lax.rsqrt(ms + NORM_EPS) * g_ref[...]


def _final_norm(x, g, bm):
    m, d = x.shape
    return pl.pallas_call(
        _final_norm_body,
        grid=(m // bm,),
        in_specs=[pl.BlockSpec((bm, d), lambda i: (i, 0)), pl.BlockSpec((1, d), lambda i: (0, 0))],
        out_specs=pl.BlockSpec((bm, d), lambda i: (i, 0)),
        out_shape=jax.ShapeDtypeStruct((m, d), f32),
        compiler_params=_params(("parallel",)),
        name="final_norm",
    )(x, g.reshape(1, d))


def _tiles(seq):
    bm = min(1024, seq)
    return dict(bm=bm, bn=512, bn_cast=256, bn_deep=256, bt=min(1024, seq),
                rope_rows=min(512, seq), norm_rows=min(256, seq))


def kernel(x, positions, norm1_g, w_in, attn_sinks, lb_param, hgrn_norm_g, p_attn, p_hgrn, w_out,
           norm2_g, w_ffn_in, conv_w, conv_b, w_down, final_norm_g):
    batch, seq, d = x.shape
    depth = w_in.shape[0]
    t = _tiles(seq)
    bm, bn = t["bm"], t["bn"]
    p_attn, p_hgrn, w_out, w_down = (w.astype(bf16) for w in (p_attn, p_hgrn, w_out, w_down))
    outs = []
    for bi in range(batch):
        xs = x[bi]
        cos, sin = _rope_tables(positions[bi], t["rope_rows"])
        for l in range(depth):
            proj = _norm_proj(xs, norm1_g[l], w_in, cos, sin, l, bm, t["bn_cast"])
            a_out = _attention(proj, attn_sinks[l])
            b_out = _hgrn(proj, lb_param, hgrn_norm_g[l], l, t["bt"])
            merged = _merge(a_out, b_out, p_attn, p_hgrn, proj, l, bm, bn)
            xs = _resid_matmul(merged, w_out, xs, l, bm, bn)
            act = _ffn_in(xs, norm2_g[l], w_ffn_in, conv_w, conv_b[l], l, bm, t["bn_cast"])
            xs = _resid_matmul(act, w_down, xs, l, bm, t["bn_deep"])
        outs.append(_final_norm(xs, final_norm_g, t["norm_rows"]))
    return jnp.stack(outs, axis=0)
```

```python
import functools

import jax
import jax.numpy as jnp
from jax import lax
from jax.experimental import pallas as pl
from jax.experimental.pallas import tpu as pltpu

N_Q_HEADS = 32
N_KV_HEADS = 4
HEAD_DIM = 64
Q_GROUP = N_Q_HEADS // N_KV_HEADS
WINDOW = 128
ROPE_THETA = 10000.0
ATTN_W = N_Q_HEADS * HEAD_DIM
KV_W = N_KV_HEADS * HEAD_DIM
HGRN_HEADS = 16
HGRN_DK = 128
HGRN_DV = 128
HGRN_FW = HGRN_HEADS * HGRN_DK
HGRN_IW = HGRN_HEADS * HGRN_DV
CONV_WIDTH = 3
NORM_EPS = 1e-5
MASK_VALUE = -1e30
MIN_FORGET = 1e-30
MAX_LB = 0.999
LOG2_E = 1.4426950408889634

COL_Q = 0
COL_K = COL_Q + ATTN_W
COL_V = COL_K + KV_W
COL_HQ = COL_V + KV_W
COL_HF = COL_HQ + HGRN_FW
COL_HI = COL_HF + HGRN_FW
COL_HG = COL_HI + HGRN_IW
COL_GA = COL_HG + HGRN_IW

LANES = 128
BF16_ROWS = 16
VMEM_LIMIT = 56 * 1024 * 1024
HGRN_CHUNK = 16
HGRN_SLAB = 8
HGRN_OUT_SLAB = 16
NORM_ROWS = 64
FFN_ROW_SPLIT = 4

f32 = jnp.float32
bf16 = jnp.bfloat16


def _params(sem):
    return pltpu.CompilerParams(dimension_semantics=sem, vmem_limit_bytes=VMEM_LIMIT)


def _norm_rows_into(x_ref, g_ref, h_ref, n_rows, dst_off):
    step = min(NORM_ROWS, n_rows)

    def body(c, carry):
        r0 = pl.multiple_of(c * step, step)
        x = x_ref[pl.ds(r0, step), :]
        ms = jnp.mean(x * x, axis=-1, keepdims=True)
        y = x * lax.rsqrt(ms + NORM_EPS) * g_ref[...]
        h_ref[pl.ds(dst_off + r0, step), :] = y.astype(h_ref.dtype)
        return carry

    lax.fori_loop(0, n_rows // step, body, 0)


def _proj_body(x_ref, g_ref, w_ref, cos_ref, sin_ref, o_ref, h_ref):
    j = pl.program_id(1)
    bn = o_ref.shape[1]

    @pl.when(j == 0)
    def _():
        _norm_rows_into(x_ref, g_ref, h_ref, x_ref.shape[0], 0)

    w = w_ref[...].astype(bf16)
    acc = jnp.dot(h_ref[...], w, preferred_element_type=f32)

    rope_blocks = -(-COL_V // bn)

    @pl.when(j < rope_blocks)
    def _():
        half = HEAD_DIM // 2
        first_half = (lax.broadcasted_iota(jnp.int32, cos_ref.shape, 1) % HEAD_DIM) < half
        for c in range(bn // LANES):
            col = j * bn + c * LANES
            scale = jnp.where(col < COL_K, HEAD_DIM ** -0.5, 1.0)
            rotate = col < COL_V
            cos = jnp.where(rotate, cos_ref[...] * scale, 1.0)
            sin = jnp.where(rotate, sin_ref[...] * scale, 0.0)
            t = acc[:, c * LANES:(c + 1) * LANES]
            fwd = pltpu.roll(t, LANES - half, axis=1)
            bwd = pltpu.roll(t, half, axis=1)
            o_ref[:, c * LANES:(c + 1) * LANES] = (
                t * cos + jnp.where(first_half, fwd, bwd) * sin).astype(o_ref.dtype)

    @pl.when(j >= rope_blocks)
    def _():
        o_ref[...] = acc.astype(o_ref.dtype)


def _norm_proj(x, g, w, cos, sin, layer, bm, bn):
    m, d = x.shape
    n = w.shape[2]
    return pl.pallas_call(
        _proj_body,
        grid=(m // bm, n // bn),
        in_specs=[
            pl.BlockSpec((bm, d), lambda i, j: (i, 0), pipeline_mode=pl.Buffered(1)),
            pl.BlockSpec((1, d), lambda i, j: (0, 0)),
            pl.BlockSpec((None, d, bn), lambda i, j: (layer, 0, j)),
            pl.BlockSpec((bm, LANES), lambda i, j: (i, 0)),
            pl.BlockSpec((bm, LANES), lambda i, j: (i, 0)),
        ],
        out_specs=pl.BlockSpec((bm, bn), lambda i, j: (i, j)),
        out_shape=jax.ShapeDtypeStruct((m, n), bf16),
        scratch_shapes=[pltpu.VMEM((bm, d), bf16)],
        compiler_params=_params(("parallel", "arbitrary")),
        name="norm_proj",
    )(x, g.reshape(1, d), w, cos, sin)


def _rope_body(pos_ref, inv_ref, cos_ref, sin_ref):
    ang = pos_ref[...].astype(f32) * inv_ref[...]
    lane = lax.broadcasted_iota(jnp.int32, ang.shape, 1)
    first_half = (lane % HEAD_DIM) < (HEAD_DIM // 2)
    cos_ref[...] = jnp.cos(ang)
    s = jnp.sin(ang)
    sin_ref[...] = jnp.where(first_half, -s, s)


def _rope_tables(positions, bs):
    s = positions.shape[-1]
    half = HEAD_DIM // 2
    inv_freq = ROPE_THETA ** (-jnp.arange(0, HEAD_DIM, 2, dtype=f32) / HEAD_DIM)
    inv_row = jnp.tile(inv_freq, LANES // half).reshape(1, LANES)
    return pl.pallas_call(
        _rope_body,
        grid=(s // bs,),
        in_specs=[pl.BlockSpec((bs, 1), lambda i: (i, 0)),
                  pl.BlockSpec((1, LANES), lambda i: (0, 0))],
        out_specs=[pl.BlockSpec((bs, LANES), lambda i: (i, 0))] * 2,
        out_shape=[jax.ShapeDtypeStruct((s, LANES), f32)] * 2,
        compiler_params=_params(("parallel",)),
        name="rope_tables",
    )(positions.reshape(s, 1), inv_row)


def _dup_head(pair, odd, low_lanes):
    swapped = pltpu.roll(pair, HEAD_DIM, axis=1)
    take_own = low_lanes != odd
    return jnp.where(take_own, pair, swapped)


def _attn_body(sink_ref, q_ref, kc_ref, kp_ref, vc_ref, vp_ref, o_ref, qs_ref, lg_ref, p_ref):
    n = pl.program_id(0)
    w = WINDOW
    pairs = Q_GROUP // 2
    low = lax.broadcasted_iota(jnp.int32, (w, LANES), 1) < HEAD_DIM
    low2 = lax.broadcasted_iota(jnp.int32, (2 * w, LANES), 1) < HEAD_DIM
    key_row = lax.broadcasted_iota(jnp.int32, (2 * w, LANES), 0)

    qi = lax.broadcasted_iota(jnp.int32, (w, 2 * w), 0)
    sj = lax.broadcasted_iota(jnp.int32, (w, 2 * w), 1)
    dist = qi + w - sj
    mask = (dist >= 0) & (dist < w) & ((n > 0) | (sj >= w))
    sink_col = lax.broadcasted_iota(jnp.int32, (1, 2 * w), 1) == 0

    for kg in range(N_KV_HEADS):
        c = kg // 2
        odd = (kg % 2) == 1
        cols = slice(c * LANES, (c + 1) * LANES)
        kband = jnp.concatenate([kp_ref[:, cols], kc_ref[:, cols]], axis=0).astype(f32)
        kk = _dup_head(kband, odd, low2).astype(bf16)
        vband = jnp.concatenate([vp_ref[:, cols], vc_ref[:, cols]], axis=0).astype(f32)
        vv = jnp.where(key_row == 0, 0.0, _dup_head(vband, odd, low2))
        ones = jnp.ones((2 * w, LANES), f32)
        v_even = jnp.concatenate([jnp.where(low2, vv, 0.0), jnp.where(low2, ones, 0.0)], axis=1).astype(bf16)
        v_odd = jnp.concatenate([jnp.where(low2, 0.0, vv), jnp.where(low2, 0.0, ones)], axis=1).astype(bf16)

        for p in range(pairs):
            col = kg * Q_GROUP * HEAD_DIM + p * LANES
            qp = q_ref[:, col:col + LANES]
            zero = jnp.zeros_like(qp)
            qs_ref[p * w:(p + 1) * w, :] = jnp.where(low, qp, zero)
            qs_ref[(pairs + p) * w:(pairs + p + 1) * w, :] = jnp.where(low, zero, qp)

        lg_ref[...] = lax.dot_general(qs_ref[...], kk, (((1,), (1,)), ((), ())),
                                      preferred_element_type=f32)
        for e in range(2):
            for p in range(pairs):
                slab = e * pairs + p
                rows = slice(slab * w, (slab + 1) * w)
                fill = jnp.where(sink_col, sink_ref[kg * Q_GROUP + 2 * p + e], MASK_VALUE)
                lg = jnp.where(mask, lg_ref[rows, :], fill)
                m = jnp.max(lg, axis=-1, keepdims=True)
                p_ref[rows, :] = jnp.exp(lg - m).astype(bf16)
        half_rows = pairs * w
        r = (jnp.dot(p_ref[0:half_rows, :], v_even, preferred_element_type=f32)
             + jnp.dot(p_ref[half_rows:, :], v_odd, preferred_element_type=f32))
        o = (r[:, 0:LANES] / r[:, LANES:]).astype(o_ref.dtype)
        for p in range(pairs):
            col = kg * Q_GROUP * HEAD_DIM + p * LANES
            o_ref[:, col:col + LANES] = o[p * w:(p + 1) * w, :]


def _attention(proj, sinks):
    s = proj.shape[0]
    w = WINDOW
    nb = s // w
    kcol, vcol = COL_K // KV_W, COL_V // KV_W
    grid_spec = pltpu.PrefetchScalarGridSpec(
        num_scalar_prefetch=1,
        grid=(nb,),
        in_specs=[
            pl.BlockSpec((w, ATTN_W), lambda n, sk: (n, 0)),
            pl.BlockSpec((w, KV_W), lambda n, sk: (n, kcol)),
            pl.BlockSpec((w, KV_W), lambda n, sk: (jnp.maximum(n - 1, 0), kcol)),
            pl.BlockSpec((w, KV_W), lambda n, sk: (n, vcol)),
            pl.BlockSpec((w, KV_W), lambda n, sk: (jnp.maximum(n - 1, 0), vcol)),
        ],
        out_specs=pl.BlockSpec((w, ATTN_W), lambda n, sk: (n, 0)),
        scratch_shapes=[
            pltpu.VMEM((Q_GROUP * w, LANES), bf16),
            pltpu.VMEM((Q_GROUP * w, 2 * w), f32),
            pltpu.VMEM((Q_GROUP * w, 2 * w), bf16),
        ],
    )
    return pl.pallas_call(
        _attn_body,
        grid_spec=grid_spec,
        out_shape=jax.ShapeDtypeStruct((s, ATTN_W), bf16),
        compiler_params=_params(("parallel",)),
        name="swa_attention",
    )(sinks.astype(f32), proj, proj, proj, proj, proj)


def _cumsum_chunks(g, row_in_chunk):
    b = g
    shift = 1
    while shift < HGRN_CHUNK:
        b = b + jnp.where(row_in_chunk >= shift, pltpu.roll(b, shift, axis=0), 0.0)
        shift *= 2
    return b


def _hgrn_body(q_ref, z_ref, i_ref, og_ref, lbp_ref, ng_ref, o_ref,
               st_ref, qd_ref, stb_ref, upd_ref, dec_ref, sc_ref, bk_ref, *, layer):
    c = HGRN_CHUNK
    hc = c // 2
    ga = HGRN_SLAB
    bt = q_ref.shape[0]
    n_chunks = bt // c
    slab = ga * c

    @pl.when(pl.program_id(1) == 0)
    def _():
        st_ref[...] = jnp.zeros_like(st_ref)

    lbp = lbp_ref[...]
    e = jnp.exp(lbp - jnp.max(lbp, axis=0, keepdims=True))
    sm = e / jnp.sum(e, axis=0, keepdims=True)
    cum = sm[0:1, :]
    for l in range(1, layer + 1):
        cum = cum + sm[l:l + 1, :]
    lower = jnp.clip(cum - sm[0:1, :], 0.0, MAX_LB)
    qscale = HGRN_DK ** -0.5
    row_in_chunk = lax.broadcasted_iota(jnp.int32, (slab, LANES), 0) % c
    upper = row_in_chunk >= hc
    half_id = lax.broadcasted_iota(jnp.int32, (2 * ga, hc, LANES), 0) % 2
    col = lax.broadcasted_iota(jnp.int32, (2 * ga, hc, LANES), 2) - hc * half_id
    tr = lax.broadcasted_iota(jnp.int32, (ga, c, c), 1)
    ts = lax.broadcasted_iota(jnp.int32, (ga, c, c), 2)
    causal = tr >= ts

    def phase_a(si, carry):
        r0 = pl.multiple_of(si * slab, slab)
        rows = pl.ds(r0, slab)
        q = q_ref[rows, :].astype(f32) * qscale
        z = z_ref[rows, :].astype(f32)
        f = lower + (1.0 - lower) * jax.nn.sigmoid(z)
        g = jnp.log(jnp.maximum(f, MIN_FORGET))
        k = 1.0 - f
        b = _cumsum_chunks(g, row_in_chunk) * LOG2_E
        bc = b.reshape(ga, c, LANES)
        b_last = bc[:, c - 1:c, :]
        b_mid = bc[:, hc - 1:hc, :]
        b_last_r = jnp.broadcast_to(b_last, (ga, c, LANES)).reshape(slab, LANES)
        b_mid_r = jnp.broadcast_to(b_mid, (ga, c, LANES)).reshape(slab, LANES)
        vb = i_ref[rows, :]

        qd_ref[rows, :] = (q * jnp.exp2(b)).astype(bf16)
        kd = (k * jnp.exp2(b_last_r - b)).astype(bf16)
        qt = jnp.where(upper, q * jnp.exp2(b - b_mid_r), 0.0).astype(bf16).reshape(ga, c, LANES)
        kt = jnp.where(upper, 0.0, k * jnp.exp2(b_mid_r - b)).astype(bf16).reshape(ga, c, LANES)
        a_off = jnp.einsum("gtk,gsk->gts", qt, kt, preferred_element_type=f32)

        q3 = q.reshape(2 * ga, hc, LANES)
        b3 = b.reshape(2 * ga, hc, LANES)
        bk_ref[0] = b3
        bk_ref[1] = k.reshape(2 * ga, hc, LANES)
        a_diag = jnp.zeros((2 * ga, hc, LANES), f32)
        for s in range(hc):
            w = jnp.exp2(b3 - bk_ref[0, :, s:s + 1, :])
            a = jnp.sum(q3 * bk_ref[1, :, s:s + 1, :] * w, axis=-1, keepdims=True)
            a_diag = jnp.where(col == s, a, a_diag)
        scores = a_diag.reshape(ga, c, LANES)[:, :, 0:c] + a_off
        sc_ref[rows, :] = jnp.where(causal, scores, 0.0).astype(bf16).reshape(slab, c)

        for gi in range(ga):
            upd_ref[si * ga + gi] = lax.dot_general(
                vb[gi * c:(gi + 1) * c, :], kd[gi * c:(gi + 1) * c, :],
                (((0,), (0,)), ((), ())), preferred_element_type=f32)
        dec_ref[pl.ds(si * ga, ga), :, :] = jnp.exp2(b_last)
        return carry

    lax.fori_loop(0, bt // slab, phase_a, 0)

    def phase_b(ci, st):
        stb_ref[ci] = st.astype(bf16)
        return st * dec_ref[ci] + upd_ref[ci]

    st_ref[...] = lax.fori_loop(0, n_chunks, phase_b, st_ref[...], unroll=4)

    ng = ng_ref[...]
    gc = HGRN_OUT_SLAB
    oslab = gc * c

    def phase_c(si, carry):
        r0 = pl.multiple_of(si * oslab, oslab)
        rows = pl.ds(r0, oslab)
        outs = []
        for gi in range(gc):
            cr = pl.ds(r0 + gi * c, c)
            inter = lax.dot_general(qd_ref[cr, :], stb_ref[si * gc + gi],
                                    (((1,), (1,)), ((), ())), preferred_element_type=f32)
            outs.append(inter + jnp.dot(sc_ref[cr, :], i_ref[cr, :], preferred_element_type=f32))
        o = jnp.concatenate(outs, axis=0)
        o = o * lax.rsqrt(jnp.mean(o * o, axis=-1, keepdims=True) + NORM_EPS) * ng
        og = og_ref[rows, :].astype(f32)
        o_ref[rows, :] = (o * (og * jax.nn.sigmoid(og))).astype(o_ref.dtype)
        return carry

    lax.fori_loop(0, bt // oslab, phase_c, 0)


def _hgrn(proj, lb_param, norm_g, layer, bt):
    s = proj.shape[0]
    depth = lb_param.shape[0]
    n_chunks = bt // HGRN_CHUNK
    cq, cf, ci, cg = (COL_HQ // LANES, COL_HF // LANES, COL_HI // LANES, COL_HG // LANES)
    return pl.pallas_call(
        functools.partial(_hgrn_body, layer=layer),
        grid=(HGRN_HEADS, s // bt),
        in_specs=[
            pl.BlockSpec((bt, LANES), lambda h, t: (t, cq + h)),
            pl.BlockSpec((bt, LANES), lambda h, t: (t, cf + h)),
            pl.BlockSpec((bt, LANES), lambda h, t: (t, ci + h)),
            pl.BlockSpec((bt, LANES), lambda h, t: (t, cg + h)),
            pl.BlockSpec((depth, LANES), lambda h, t: (0, h)),
            pl.BlockSpec((1, LANES), lambda h, t: (0, h)),
        ],
        out_specs=pl.BlockSpec((bt, LANES), lambda h, t: (t, h)),
        out_shape=jax.ShapeDtypeStruct((s, HGRN_IW), bf16),
        scratch_shapes=[
            pltpu.VMEM((HGRN_DV, HGRN_DK), f32),
            pltpu.VMEM((bt, LANES), bf16),
            pltpu.VMEM((n_chunks, HGRN_DV, HGRN_DK), bf16),
            pltpu.VMEM((n_chunks, HGRN_DV, HGRN_DK), f32),
            pltpu.VMEM((n_chunks, 1, LANES), f32),
            pltpu.VMEM((bt, HGRN_CHUNK), bf16),
            pltpu.VMEM((2, 2 * HGRN_SLAB, HGRN_CHUNK // 2, LANES), f32),
        ],
        compiler_params=_params(("parallel", "arbitrary")),
        name="hgrn2",
    )(proj, proj, proj, proj, lb_param.astype(f32), norm_g.reshape(1, -1).astype(f32))


def _merge_body(a_ref, b_ref, pa_ref, ph_ref, ga_ref, gb_ref, o_ref):
    ya = jnp.dot(a_ref[...], pa_ref[...], preferred_element_type=f32)
    yb = jnp.dot(b_ref[...], ph_ref[...], preferred_element_type=f32)
    ga = jax.nn.sigmoid(ga_ref[...].astype(f32))
    gb = jax.nn.sigmoid(gb_ref[...].astype(f32))
    o_ref[...] = (ga * ya + gb * yb).astype(o_ref.dtype)


def _merge(a, b, pa, ph, proj, layer, bm, bn):
    m, ka = a.shape
    kb = b.shape[1]
    d = pa.shape[2]
    ca = COL_GA // bn
    cb = (COL_GA + d) // bn
    return pl.pallas_call(
        _merge_body,
        grid=(m // bm, d // bn),
        in_specs=[
            pl.BlockSpec((bm, ka), lambda i, j: (i, 0)),
            pl.BlockSpec((bm, kb), lambda i, j: (i, 0)),
            pl.BlockSpec((None, ka, bn), lambda i, j: (layer, 0, j)),
            pl.BlockSpec((None, kb, bn), lambda i, j: (layer, 0, j)),
            pl.BlockSpec((bm, bn), lambda i, j: (i, ca + j)),
            pl.BlockSpec((bm, bn), lambda i, j: (i, cb + j)),
        ],
        out_specs=pl.BlockSpec((bm, bn), lambda i, j: (i, j)),
        out_shape=jax.ShapeDtypeStruct((m, d), bf16),
        compiler_params=_params(("parallel", "arbitrary")),
        name="gated_merge",
    )(a, b, pa, ph, proj, proj)


def _resid_body(a_ref, w_ref, x_ref, o_ref):
    o_ref[...] = x_ref[...] + jnp.dot(a_ref[...], w_ref[...], preferred_element_type=f32)


def _resid_matmul(a, w, x, layer, bm, bn):
    m, k = a.shape
    d = w.shape[2]
    return pl.pallas_call(
        _resid_body,
        grid=(m // bm, d // bn),
        in_specs=[
            pl.BlockSpec((bm, k), lambda i, j: (i, 0)),
            pl.BlockSpec((None, k, bn), lambda i, j: (layer, 0, j)),
            pl.BlockSpec((bm, bn), lambda i, j: (i, j)),
        ],
        out_specs=pl.BlockSpec((bm, bn), lambda i, j: (i, j)),
        out_shape=jax.ShapeDtypeStruct((m, d), f32),
        input_output_aliases={2: 0},
        compiler_params=_params(("parallel", "arbitrary")),
        name="resid_matmul",
    )(a, w, x)


def _ffn_in_body(x_ref, xh_ref, g_ref, wu_ref, wg_ref, cw_ref, cb_ref, o_ref, h_ref, u_ref):
    bm = x_ref.shape[0]
    halo = BF16_ROWS

    @pl.when(pl.program_id(1) == 0)
    def _():
        _norm_rows_into(x_ref, g_ref, h_ref, bm, halo)

        @pl.when(pl.program_id(0) == 0)
        def _():
            h_ref[0:halo, :] = jnp.zeros((halo, h_ref.shape[1]), h_ref.dtype)

        @pl.when(pl.program_id(0) > 0)
        def _():
            _norm_rows_into(xh_ref, g_ref, h_ref, halo, 0)

    wu = wu_ref[...].astype(bf16)
    wg = wg_ref[...].astype(bf16)
    cw = cw_ref[...]
    cb = cb_ref[...]
    rb = bm // FFN_ROW_SPLIT
    for r in range(FFN_ROW_SPLIT):
        lo = halo + r * rb
        first = 0 if r == 0 else lo
        u_ref[first:lo + rb, :] = jnp.dot(h_ref[first:lo + rb, :], wu, preferred_element_type=f32)
        gate = jnp.dot(h_ref[lo:lo + rb, :], wg, preferred_element_type=f32)
        u = cb + cw[CONV_WIDTH - 1:CONV_WIDTH, :] * u_ref[lo:lo + rb, :]
        for j in range(CONV_WIDTH - 1):
            back = CONV_WIDTH - 1 - j
            u = u + cw[j:j + 1, :] * u_ref[lo - back:lo - back + rb, :]
        gelu = 0.5 * u * (1.0 + lax.erf(u * (2.0 ** -0.5)))
        o_ref[r * rb:(r + 1) * rb, :] = (gelu * gate).astype(o_ref.dtype)


def _ffn_in(x, g, w, conv_w, conv_b, layer, bm, bn):
    m, d = x.shape
    ff = conv_w.shape[2]
    halo = BF16_ROWS
    nj = ff // bn
    return pl.pallas_call(
        _ffn_in_body,
        grid=(m // bm, nj),
        in_specs=[
            pl.BlockSpec((bm, d), lambda i, j: (i, 0), pipeline_mode=pl.Buffered(1)),
            pl.BlockSpec((halo, d), lambda i, j: (jnp.maximum(i * (bm // halo) - 1, 0), 0)),
            pl.BlockSpec((1, d), lambda i, j: (0, 0)),
            pl.BlockSpec((None, d, bn), lambda i, j: (layer, 0, j)),
            pl.BlockSpec((None, d, bn), lambda i, j: (layer, 0, nj + j)),
            pl.BlockSpec((None, CONV_WIDTH, bn), lambda i, j: (layer, 0, j)),
            pl.BlockSpec((1, bn), lambda i, j: (0, j)),
        ],
        out_specs=pl.BlockSpec((bm, bn), lambda i, j: (i, j)),
        out_shape=jax.ShapeDtypeStruct((m, ff), bf16),
        scratch_shapes=[pltpu.VMEM((bm + halo, d), bf16), pltpu.VMEM((bm + halo, bn), f32)],
        compiler_params=_params(("parallel", "arbitrary")),
        name="ffn_in",
    )(x, x, g.reshape(1, d), w, w, conv_w, conv_b.reshape(1, ff))


def _final_norm_body(x_ref, g_ref, o_ref):
    x = x_ref[...]
    ms = jnp.mean(x * x, axis=-1, keepdims=True)
    o_ref[...] = x * lax.rsqrt(ms + NORM_EPS) * g_ref[...]


def _final_norm(x, g, bm):
    m, d = x.shape
    return pl.pallas_call(
        _final_norm_body,
        grid=(m // bm,),
        in_specs=[pl.BlockSpec((bm, d), lambda i: (i, 0)), pl.BlockSpec((1, d), lambda i: (0, 0))],
        out_specs=pl.BlockSpec((bm, d), lambda i: (i, 0)),
        out_shape=jax.ShapeDtypeStruct((m, d), f32),
        compiler_params=_params(("parallel",)),
        name="final_norm",
    )(x, g.reshape(1, d))


def _tiles(seq):
    bm = min(1024, seq)
    return dict(bm=bm, bn=512, bn_ffn=256, bn_deep=256, bt=min(1024, seq),
                rope_rows=min(512, seq), norm_rows=min(256, seq))


def kernel(x, positions, norm1_g, w_in, attn_sinks, lb_param, hgrn_norm_g, p_attn, p_hgrn, w_out,
           norm2_g, w_ffn_in, conv_w, conv_b, w_down, final_norm_g):
    batch, seq, d = x.shape
    depth = w_in.shape[0]
    t = _tiles(seq)
    bm, bn = t["bm"], t["bn"]
    p_attn, p_hgrn, w_out, w_down = (w.astype(bf16) for w in (p_attn, p_hgrn, w_out, w_down))
    outs = []
    for bi in range(batch):
        xs = x[bi]
        cos, sin = _rope_tables(positions[bi], t["rope_rows"])
        for l in range(depth):
            proj = _norm_proj(xs, norm1_g[l], w_in, cos, sin, l, bm, bn)
            a_out = _attention(proj, attn_sinks[l])
            b_out = _hgrn(proj, lb_param, hgrn_norm_g[l], l, t["bt"])
            merged = _merge(a_out, b_out, p_attn, p_hgrn, proj, l, bm, bn)
            xs = _resid_matmul(merged, w_out, xs, l, bm, bn)
            act = _ffn_in(xs, norm2_g[l], w_ffn_in, conv_w, conv_b[l], l, bm, t["bn_ffn"])
            xs = _resid_matmul(act, w_down, xs, l, bm, t["bn_deep"])
        outs.append(_final_norm(xs, final_norm_g, t["norm_rows"]))
    return jnp.stack(outs, axis=0)
```

```python
import functools

import jax
import jax.numpy as jnp
from jax import lax
from jax.experimental import pallas as pl
from jax.experimental.pallas import tpu as pltpu

N_Q_HEADS = 32
N_KV_HEADS = 4
HEAD_DIM = 64
Q_GROUP = N_Q_HEADS // N_KV_HEADS
WINDOW = 128
ROPE_THETA = 10000.0
ATTN_W = N_Q_HEADS * HEAD_DIM
KV_W = N_KV_HEADS * HEAD_DIM
HGRN_HEADS = 16
HGRN_DK = 128
HGRN_DV = 128
HGRN_FW = HGRN_HEADS * HGRN_DK
HGRN_IW = HGRN_HEADS * HGRN_DV
CONV_WIDTH = 3
NORM_EPS = 1e-5
MASK_VALUE = -1e30
MIN_FORGET = 1e-30
MAX_LB = 0.999
LOG2_E = 1.4426950408889634

COL_Q = 0
COL_K = COL_Q + ATTN_W
COL_V = COL_K + KV_W
COL_HQ = COL_V + KV_W
COL_HF = COL_HQ + HGRN_FW
COL_HI = COL_HF + HGRN_FW
COL_HG = COL_HI + HGRN_IW
COL_GA = COL_HG + HGRN_IW

LANES = 128
BF16_ROWS = 16
VMEM_LIMIT = 56 * 1024 * 1024
HGRN_CHUNK = 32
HGRN_BLOCK = 8
HGRN_SLAB = 4
HGRN_OUT_SLAB = 8
NORM_ROWS = 64

f32 = jnp.float32
bf16 = jnp.bfloat16


def _params(sem):
    return pltpu.CompilerParams(dimension_semantics=sem, vmem_limit_bytes=VMEM_LIMIT)


def _norm_rows_into(x_ref, g_ref, h_ref, n_rows, dst_off):
    step = min(NORM_ROWS, n_rows)

    def body(c, carry):
        r0 = pl.multiple_of(c * step, step)
        x = x_ref[pl.ds(r0, step), :]
        ms = jnp.mean(x * x, axis=-1, keepdims=True)
        y = x * lax.rsqrt(ms + NORM_EPS) * g_ref[...]
        h_ref[pl.ds(dst_off + r0, step), :] = y.astype(h_ref.dtype)
        return carry

    lax.fori_loop(0, n_rows // step, body, 0)


def _proj_body(x_ref, g_ref, w_ref, cos_ref, sin_ref, o_ref, h_ref):
    j = pl.program_id(1)
    bn = o_ref.shape[1]

    @pl.when(j == 0)
    def _():
        _norm_rows_into(x_ref, g_ref, h_ref, x_ref.shape[0], 0)

    w = w_ref[...].astype(bf16)
    acc = jnp.dot(h_ref[...], w, preferred_element_type=f32)

    rope_blocks = -(-COL_V // bn)

    @pl.when(j < rope_blocks)
    def _():
        half = HEAD_DIM // 2
        first_half = (lax.broadcasted_iota(jnp.int32, cos_ref.shape, 1) % HEAD_DIM) < half
        for c in range(bn // LANES):
            col = j * bn + c * LANES
            scale = jnp.where(col < COL_K, HEAD_DIM ** -0.5, 1.0)
            rotate = col < COL_V
            cos = jnp.where(rotate, cos_ref[...] * scale, 1.0)
            sin = jnp.where(rotate, sin_ref[...] * scale, 0.0)
            t = acc[:, c * LANES:(c + 1) * LANES]
            fwd = pltpu.roll(t, LANES - half, axis=1)
            bwd = pltpu.roll(t, half, axis=1)
            o_ref[:, c * LANES:(c + 1) * LANES] = (
                t * cos + jnp.where(first_half, fwd, bwd) * sin).astype(o_ref.dtype)

    @pl.when(j >= rope_blocks)
    def _():
        o_ref[...] = acc.astype(o_ref.dtype)


def _norm_proj(x, g, w, cos, sin, layer, bm, bn):
    m, d = x.shape
    n = w.shape[2]
    return pl.pallas_call(
        _proj_body,
        grid=(m // bm, n // bn),
        in_specs=[
            pl.BlockSpec((bm, d), lambda i, j: (i, 0), pipeline_mode=pl.Buffered(1)),
            pl.BlockSpec((1, d), lambda i, j: (0, 0)),
            pl.BlockSpec((None, d, bn), lambda i, j: (layer, 0, j)),
            pl.BlockSpec((bm, LANES), lambda i, j: (i, 0)),
            pl.BlockSpec((bm, LANES), lambda i, j: (i, 0)),
        ],
        out_specs=pl.BlockSpec((bm, bn), lambda i, j: (i, j)),
        out_shape=jax.ShapeDtypeStruct((m, n), bf16),
        scratch_shapes=[pltpu.VMEM((bm, d), bf16)],
        compiler_params=_params(("parallel", "arbitrary")),
        name="norm_proj",
    )(x, g.reshape(1, d), w, cos, sin)


def _rope_body(pos_ref, inv_ref, cos_ref, sin_ref):
    ang = pos_ref[...].astype(f32) * inv_ref[...]
    lane = lax.broadcasted_iota(jnp.int32, ang.shape, 1)
    first_half = (lane % HEAD_DIM) < (HEAD_DIM // 2)
    cos_ref[...] = jnp.cos(ang)
    s = jnp.sin(ang)
    sin_ref[...] = jnp.where(first_half, -s, s)


def _rope_tables(positions, bs):
    s = positions.shape[-1]
    half = HEAD_DIM // 2
    inv_freq = ROPE_THETA ** (-jnp.arange(0, HEAD_DIM, 2, dtype=f32) / HEAD_DIM)
    inv_row = jnp.tile(inv_freq, LANES // half).reshape(1, LANES)
    return pl.pallas_call(
        _rope_body,
        grid=(s // bs,),
        in_specs=[pl.BlockSpec((bs, 1), lambda i: (i, 0)),
                  pl.BlockSpec((1, LANES), lambda i: (0, 0))],
        out_specs=[pl.BlockSpec((bs, LANES), lambda i: (i, 0))] * 2,
        out_shape=[jax.ShapeDtypeStruct((s, LANES), f32)] * 2,
        compiler_params=_params(("parallel",)),
        name="rope_tables",
    )(positions.reshape(s, 1), inv_row)


def _dup_head(pair, odd, low_lanes):
    swapped = pltpu.roll(pair, HEAD_DIM, axis=1)
    take_own = low_lanes != odd
    return jnp.where(take_own, pair, swapped)


def _attn_body(sink_ref, q_ref, kc_ref, kp_ref, vc_ref, vp_ref, o_ref, qs_ref, lg_ref, p_ref):
    n = pl.program_id(0)
    w = WINDOW
    pairs = Q_GROUP // 2
    low = lax.broadcasted_iota(jnp.int32, (w, LANES), 1) < HEAD_DIM
    low2 = lax.broadcasted_iota(jnp.int32, (2 * w, LANES), 1) < HEAD_DIM
    key_row = lax.broadcasted_iota(jnp.int32, (2 * w, LANES), 0)

    qi = lax.broadcasted_iota(jnp.int32, (w, 2 * w), 0)
    sj = lax.broadcasted_iota(jnp.int32, (w, 2 * w), 1)
    dist = qi + w - sj
    mask = (dist >= 0) & (dist < w) & ((n > 0) | (sj >= w))
    sink_col = lax.broadcasted_iota(jnp.int32, (1, 2 * w), 1) == 0

    for kg in range(N_KV_HEADS):
        c = kg // 2
        odd = (kg % 2) == 1
        cols = slice(c * LANES, (c + 1) * LANES)
        kband = jnp.concatenate([kp_ref[:, cols], kc_ref[:, cols]], axis=0).astype(f32)
        kk = _dup_head(kband, odd, low2).astype(bf16)
        vband = jnp.concatenate([vp_ref[:, cols], vc_ref[:, cols]], axis=0).astype(f32)
        vv = jnp.where(key_row == 0, 0.0, _dup_head(vband, odd, low2))
        ones = jnp.ones((2 * w, LANES), f32)
        v_even = jnp.concatenate([jnp.where(low2, vv, 0.0), jnp.where(low2, ones, 0.0)], axis=1).astype(bf16)
        v_odd = jnp.concatenate([jnp.where(low2, 0.0, vv), jnp.where(low2, 0.0, ones)], axis=1).astype(bf16)

        for p in range(pairs):
            col = kg * Q_GROUP * HEAD_DIM + p * LANES
            qp = q_ref[:, col:col + LANES]
            zero = jnp.zeros_like(qp)
            qs_ref[p * w:(p + 1) * w, :] = jnp.where(low, qp, zero)
            qs_ref[(pairs + p) * w:(pairs + p + 1) * w, :] = jnp.where(low, zero, qp)

        lg_ref[...] = lax.dot_general(qs_ref[...], kk, (((1,), (1,)), ((), ())),
                                      preferred_element_type=f32)
        for e in range(2):
            for p in range(pairs):
                slab = e * pairs + p
                rows = slice(slab * w, (slab + 1) * w)
                fill = jnp.where(sink_col, sink_ref[kg * Q_GROUP + 2 * p + e], MASK_VALUE)
                lg = jnp.where(mask, lg_ref[rows, :], fill)
                m = jnp.max(lg, axis=-1, keepdims=True)
                p_ref[rows, :] = jnp.exp(lg - m).astype(bf16)
        half_rows = pairs * w
        r = (jnp.dot(p_ref[0:half_rows, :], v_even, preferred_element_type=f32)
             + jnp.dot(p_ref[half_rows:, :], v_odd, preferred_element_type=f32))
        o = (r[:, 0:LANES] / r[:, LANES:]).astype(o_ref.dtype)
        for p in range(pairs):
            col = kg * Q_GROUP * HEAD_DIM + p * LANES
            o_ref[:, col:col + LANES] = o[p * w:(p + 1) * w, :]


def _attention(proj, sinks):
    s = proj.shape[0]
    w = WINDOW
    nb = s // w
    kcol, vcol = COL_K // KV_W, COL_V // KV_W
    grid_spec = pltpu.PrefetchScalarGridSpec(
        num_scalar_prefetch=1,
        grid=(nb,),
        in_specs=[
            pl.BlockSpec((w, ATTN_W), lambda n, sk: (n, 0)),
            pl.BlockSpec((w, KV_W), lambda n, sk: (n, kcol)),
            pl.BlockSpec((w, KV_W), lambda n, sk: (jnp.maximum(n - 1, 0), kcol)),
            pl.BlockSpec((w, KV_W), lambda n, sk: (n, vcol)),
            pl.BlockSpec((w, KV_W), lambda n, sk: (jnp.maximum(n - 1, 0), vcol)),
        ],
        out_specs=pl.BlockSpec((w, ATTN_W), lambda n, sk: (n, 0)),
        scratch_shapes=[
            pltpu.VMEM((Q_GROUP * w, LANES), bf16),
            pltpu.VMEM((Q_GROUP * w, 2 * w), f32),
            pltpu.VMEM((Q_GROUP * w, 2 * w), bf16),
        ],
    )
    return pl.pallas_call(
        _attn_body,
        grid_spec=grid_spec,
        out_shape=jax.ShapeDtypeStruct((s, ATTN_W), bf16),
        compiler_params=_params(("parallel",)),
        name="swa_attention",
    )(sinks.astype(f32), proj, proj, proj, proj, proj)


def _cumsum_chunks(g, row_in_chunk):
    b = g
    shift = 1
    while shift < HGRN_CHUNK:
        b = b + jnp.where(row_in_chunk >= shift, pltpu.roll(b, shift, axis=0), 0.0)
        shift *= 2
    return b


def _hgrn_body(q_ref, z_ref, i_ref, og_ref, lbp_ref, ng_ref, o_ref,
               st_ref, qd_ref, stb_ref, upd_ref, dec_ref, sc_ref, bk_ref, *, layer):
    c = HGRN_CHUNK
    hb = HGRN_BLOCK
    nblk = c // hb
    ga = HGRN_SLAB
    bt = q_ref.shape[0]
    n_chunks = bt // c
    slab = ga * c
    nb3 = slab // hb

    @pl.when(pl.program_id(1) == 0)
    def _():
        st_ref[...] = jnp.zeros_like(st_ref)

    lbp = lbp_ref[...]
    e = jnp.exp(lbp - jnp.max(lbp, axis=0, keepdims=True))
    sm = e / jnp.sum(e, axis=0, keepdims=True)
    cum = sm[0:1, :]
    for l in range(1, layer + 1):
        cum = cum + sm[l:l + 1, :]
    lower = jnp.clip(cum - sm[0:1, :], 0.0, MAX_LB)
    qscale = HGRN_DK ** -0.5
    row_in_chunk = lax.broadcasted_iota(jnp.int32, (slab, LANES), 0) % c
    blk_id = lax.broadcasted_iota(jnp.int32, (nb3, hb, LANES), 0) % nblk
    col = lax.broadcasted_iota(jnp.int32, (nb3, hb, LANES), 2) - hb * blk_id
    tr = lax.broadcasted_iota(jnp.int32, (ga, c, c), 1)
    ts = lax.broadcasted_iota(jnp.int32, (ga, c, c), 2)
    causal = tr >= ts

    def phase_a(si, carry):
        r0 = pl.multiple_of(si * slab, slab)
        rows = pl.ds(r0, slab)
        q = q_ref[rows, :].astype(f32) * qscale
        z = z_ref[rows, :].astype(f32)
        f = lower + (1.0 - lower) * jax.nn.sigmoid(z)
        g = jnp.log(jnp.maximum(f, MIN_FORGET))
        k = 1.0 - f
        b = _cumsum_chunks(g, row_in_chunk) * LOG2_E
        bc = b.reshape(ga, c, LANES)
        b_last = bc[:, c - 1:c, :]
        b_last_r = jnp.broadcast_to(b_last, (ga, c, LANES)).reshape(slab, LANES)
        vb = i_ref[rows, :]

        qd_ref[rows, :] = (q * jnp.exp2(b)).astype(bf16)
        kd = (k * jnp.exp2(b_last_r - b)).astype(bf16)

        q3 = q.reshape(nb3, hb, LANES)
        b3 = b.reshape(nb3, hb, LANES)
        bk_ref[0] = b3
        bk_ref[1] = k.reshape(nb3, hb, LANES)
        a_diag = jnp.zeros((nb3, hb, LANES), f32)
        for s in range(hb):
            w = jnp.exp2(b3 - bk_ref[0, :, s:s + 1, :])
            a = jnp.sum(q3 * bk_ref[1, :, s:s + 1, :] * w, axis=-1, keepdims=True)
            a_diag = jnp.where(col == s, a, a_diag)
        scores = a_diag.reshape(ga, c, LANES)[:, :, 0:c]

        group = 2 * hb
        while group <= c:
            n_groups = slab // group
            half = group // 2
            b_mid = b.reshape(n_groups, group, LANES)[:, half - 1:half, :]
            b_mid_r = jnp.broadcast_to(b_mid, (n_groups, group, LANES)).reshape(slab, LANES)
            upper = (row_in_chunk % group) >= half
            qt = jnp.where(upper, q * jnp.exp2(b - b_mid_r), 0.0).astype(bf16).reshape(ga, c, LANES)
            kt = jnp.where(upper, 0.0, k * jnp.exp2(b_mid_r - b)).astype(bf16).reshape(ga, c, LANES)
            a_off = jnp.einsum("gtk,gsk->gts", qt, kt, preferred_element_type=f32)
            if group < c:
                a_off = jnp.where((tr // group) == (ts // group), a_off, 0.0)
            scores = scores + a_off
            group *= 2
        sc_ref[rows, :] = jnp.where(causal, scores, 0.0).astype(bf16).reshape(slab, c)

        for gi in range(ga):
            upd_ref[si * ga + gi] = lax.dot_general(
                vb[gi * c:(gi + 1) * c, :], kd[gi * c:(gi + 1) * c, :],
                (((0,), (0,)), ((), ())), preferred_element_type=f32)
        dec_ref[pl.ds(si * ga, ga), :, :] = jnp.exp2(b_last)
        return carry

    lax.fori_loop(0, bt // slab, phase_a, 0)

    def phase_b(ci, st):
        stb_ref[ci] = st.astype(bf16)
        return st * dec_ref[ci] + upd_ref[ci]

    st_ref[...] = lax.fori_loop(0, n_chunks, phase_b, st_ref[...], unroll=4)

    ng = ng_ref[...]
    gc = HGRN_OUT_SLAB
    oslab = gc * c

    def phase_c(si, carry):
        r0 = pl.multiple_of(si * oslab, oslab)
        rows = pl.ds(r0, oslab)
        outs = []
        for gi in range(gc):
            cr = pl.ds(r0 + gi * c, c)
            inter = lax.dot_general(qd_ref[cr, :], stb_ref[si * gc + gi],
                                    (((1,), (1,)), ((), ())), preferred_element_type=f32)
            outs.append(inter + jnp.dot(sc_ref[cr, :], i_ref[cr, :], preferred_element_type=f32))
        o = jnp.concatenate(outs, axis=0)
        o = o * lax.rsqrt(jnp.mean(o * o, axis=-1, keepdims=True) + NORM_EPS) * ng
        og = og_ref[rows, :].astype(f32)
        o_ref[rows, :] = (o * (og * jax.nn.sigmoid(og))).astype(o_ref.dtype)
        return carry

    lax.fori_loop(0, bt // oslab, phase_c, 0)


def _hgrn(proj, lb_param, norm_g, layer, bt):
    s = proj.shape[0]
    depth = lb_param.shape[0]
    n_chunks = bt // HGRN_CHUNK
    cq, cf, ci, cg = (COL_HQ // LANES, COL_HF // LANES, COL_HI // LANES, COL_HG // LANES)
    return pl.pallas_call(
        functools.partial(_hgrn_body, layer=layer),
        grid=(HGRN_HEADS, s // bt),
        in_specs=[
            pl.BlockSpec((bt, LANES), lambda h, t: (t, cq + h)),
            pl.BlockSpec((bt, LANES), lambda h, t: (t, cf + h)),
            pl.BlockSpec((bt, LANES), lambda h, t: (t, ci + h)),
            pl.BlockSpec((bt, LANES), lambda h, t: (t, cg + h)),
            pl.BlockSpec((depth, LANES), lambda h, t: (0, h)),
            pl.BlockSpec((1, LANES), lambda h, t: (0, h)),
        ],
        out_specs=pl.BlockSpec((bt, LANES), lambda h, t: (t, h)),
        out_shape=jax.ShapeDtypeStruct((s, HGRN_IW), bf16),
        scratch_shapes=[
            pltpu.VMEM((HGRN_DV, HGRN_DK), f32),
            pltpu.VMEM((bt, LANES), bf16),
            pltpu.VMEM((n_chunks, HGRN_DV, HGRN_DK), bf16),
            pltpu.VMEM((n_chunks, HGRN_DV, HGRN_DK), f32),
            pltpu.VMEM((n_chunks, 1, LANES), f32),
            pltpu.VMEM((bt, HGRN_CHUNK), bf16),
            pltpu.VMEM((2, HGRN_SLAB * HGRN_CHUNK // HGRN_BLOCK, HGRN_BLOCK, LANES), f32),
        ],
        compiler_params=_params(("parallel", "arbitrary")),
        name="hgrn2",
    )(proj, proj, proj, proj, lb_param.astype(f32), norm_g.reshape(1, -1).astype(f32))


def _merge_body(a_ref, b_ref, pa_ref, ph_ref, ga_ref, gb_ref, o_ref):
    ya = jnp.dot(a_ref[...], pa_ref[...].astype(bf16), preferred_element_type=f32)
    yb = jnp.dot(b_ref[...], ph_ref[...].astype(bf16), preferred_element_type=f32)
    ga = jax.nn.sigmoid(ga_ref[...].astype(f32))
    gb = jax.nn.sigmoid(gb_ref[...].astype(f32))
    o_ref[...] = (ga * ya + gb * yb).astype(o_ref.dtype)


def _merge(a, b, pa, ph, proj, layer, bm, bn):
    m, ka = a.shape
    kb = b.shape[1]
    d = pa.shape[2]
    ca = COL_GA // bn
    cb = (COL_GA + d) // bn
    return pl.pallas_call(
        _merge_body,
        grid=(m // bm, d // bn),
        in_specs=[
            pl.BlockSpec((bm, ka), lambda i, j: (i, 0)),
            pl.BlockSpec((bm, kb), lambda i, j: (i, 0)),
            pl.BlockSpec((None, ka, bn), lambda i, j: (layer, 0, j)),
            pl.BlockSpec((None, kb, bn), lambda i, j: (layer, 0, j)),
            pl.BlockSpec((bm, bn), lambda i, j: (i, ca + j)),
            pl.BlockSpec((bm, bn), lambda i, j: (i, cb + j)),
        ],
        out_specs=pl.BlockSpec((bm, bn), lambda i, j: (i, j)),
        out_shape=jax.ShapeDtypeStruct((m, d), bf16),
        compiler_params=_params(("parallel", "arbitrary")),
        name="gated_merge",
    )(a, b, pa, ph, proj, proj)


def _resid_body(a_ref, w_ref, x_ref, o_ref):
    o_ref[...] = x_ref[...] + jnp.dot(a_ref[...], w_ref[...].astype(bf16), preferred_element_type=f32)


def _resid_matmul(a, w, x, layer, bm, bn, in_place):
    m, k = a.shape
    d = w.shape[2]
    return pl.pallas_call(
        _resid_body,
        grid=(m // bm, d // bn),
        in_specs=[
            pl.BlockSpec((bm, k), lambda i, j: (i, 0)),
            pl.BlockSpec((None, k, bn), lambda i, j: (layer, 0, j)),
            pl.BlockSpec((bm, bn), lambda i, j: (i, j)),
        ],
        out_specs=pl.BlockSpec((bm, bn), lambda i, j: (i, j)),
        out_shape=jax.ShapeDtypeStruct((m, d), f32),
        input_output_aliases={2: 0} if in_place else {},
        compiler_params=_params(("parallel", "arbitrary")),
        name="resid_matmul",
    )(a, w, x)


def _ffn_in_body(x_ref, xh_ref, g_ref, wu_ref, wg_ref, cw_ref, cb_ref, o_ref, h_ref, u_ref):
    bm = x_ref.shape[0]
    halo = BF16_ROWS

    @pl.when(pl.program_id(1) == 0)
    def _():
        _norm_rows_into(x_ref, g_ref, h_ref, bm, halo)

        @pl.when(pl.program_id(0) == 0)
        def _():
            h_ref[0:halo, :] = jnp.zeros((halo, h_ref.shape[1]), h_ref.dtype)

        @pl.when(pl.program_id(0) > 0)
        def _():
            _norm_rows_into(xh_ref, g_ref, h_ref, halo, 0)

    u_ref[...] = jnp.dot(h_ref[...], wu_ref[...].astype(bf16), preferred_element_type=f32)
    gate = jnp.dot(h_ref[halo:, :], wg_ref[...].astype(bf16), preferred_element_type=f32)
    cw = cw_ref[...]
    u = cb_ref[...] + cw[CONV_WIDTH - 1:CONV_WIDTH, :] * u_ref[halo:, :]
    for j in range(CONV_WIDTH - 1):
        back = CONV_WIDTH - 1 - j
        u = u + cw[j:j + 1, :] * u_ref[pl.ds(halo - back, bm), :]
    gelu = 0.5 * u * (1.0 + lax.erf(u * (2.0 ** -0.5)))
    o_ref[...] = (gelu * gate).astype(o_ref.dtype)


def _ffn_in(x, g, w, conv_w, conv_b, layer, bm, bn):
    m, d = x.shape
    ff = conv_w.shape[2]
    halo = BF16_ROWS
    nj = ff // bn
    return pl.pallas_call(
        _ffn_in_body,
        grid=(m // bm, nj),
        in_specs=[
            pl.BlockSpec((bm, d), lambda i, j: (i, 0), pipeline_mode=pl.Buffered(1)),
            pl.BlockSpec((halo, d), lambda i, j: (jnp.maximum(i * (bm // halo) - 1, 0), 0)),
            pl.BlockSpec((1, d), lambda i, j: (0, 0)),
            pl.BlockSpec((None, d, bn), lambda i, j: (layer, 0, j)),
            pl.BlockSpec((None, d, bn), lambda i, j: (layer, 0, nj + j)),
            pl.BlockSpec((None, CONV_WIDTH, bn), lambda i, j: (layer, 0, j)),
            pl.BlockSpec((1, bn), lambda i, j: (0, j)),
        ],
        out_specs=pl.BlockSpec((bm, bn), lambda i, j: (i, j)),
        out_shape=jax.ShapeDtypeStruct((m, ff), bf16),
        scratch_shapes=[pltpu.VMEM((bm + halo, d), bf16), pltpu.VMEM((bm + halo, bn), f32)],
        compiler_params=_params(("parallel", "arbitrary")),
        name="ffn_in",
    )(x, x, g.reshape(1, d), w, w, conv_w, conv_b.reshape(1, ff))


def _final_norm_body(x_ref, g_ref, o_ref):
    x = x_ref[...]
    ms = jnp.mean(x * x, axis=-1, keepdims=True)
    o_ref[...] = x * lax.rsqrt(ms + NORM_EPS) * g_ref[...]


def _final_norm(x, g, bm):
    m, d = x.shape
    return pl.pallas_call(
        _final_norm_body,
        grid=(m // bm,),
        in_specs=[pl.BlockSpec((bm, d), lambda i: (i, 0)), pl.BlockSpec((1, d), lambda i: (0, 0))],
        out_specs=pl.BlockSpec((bm, d), lambda i: (i, 0)),
        out_shape=jax.ShapeDtypeStruct((m, d), f32),
        compiler_params=_params(("parallel",)),
        name="final_norm",
    )(x, g.reshape(1, d))


def _tiles(seq):
    bm = min(1024, seq)
    return dict(bm=bm, bn=512, bn_ffn=256, bn_deep=256, bt=min(1024, seq),
                rope_rows=min(512, seq), norm_rows=min(256, seq))


def kernel(x, positions, norm1_g, w_in, attn_sinks, lb_param, hgrn_norm_g, p_attn, p_hgrn, w_out,
           norm2_g, w_ffn_in, conv_w, conv_b, w_down, final_norm_g):
    batch, seq, d = x.shape
    depth = w_in.shape[0]
    t = _tiles(seq)
    bm, bn = t["bm"], t["bn"]
    w_down = w_down.astype(bf16)
    outs = []
    for bi in range(batch):
        xs = x[bi]
        cos, sin = _rope_tables(positions[bi], t["rope_rows"])
        for l in range(depth):
            proj = _norm_proj(xs, norm1_g[l], w_in, cos, sin, l, bm, bn)
            a_out = _attention(proj, attn_sinks[l])
            b_out = _hgrn(proj, lb_param, hgrn_norm_g[l], l, t["bt"])
            merged = _merge(a_out, b_out, p_attn, p_hgrn, proj, l, bm, bn)
            xs = _resid_matmul(merged, w_out, xs, l, bm, bn, in_place=l > 0)
            act = _ffn_in(xs, norm2_g[l], w_ffn_in, conv_w, conv_b[l], l, bm, t["bn_ffn"])
            xs = _resid_matmul(act, w_down, xs, l, bm, t["bn_deep"], in_place=True)
        outs.append(_final_norm(xs, final_norm_g, t["norm_rows"]))
    return jnp.stack(outs, axis=0)
```

```python
import functools

import jax
import jax.numpy as jnp
from jax import lax
from jax.experimental import pallas as pl
from jax.experimental.pallas import tpu as pltpu

N_Q_HEADS = 32
N_KV_HEADS = 4
HEAD_DIM = 64
Q_GROUP = N_Q_HEADS // N_KV_HEADS
WINDOW = 128
ROPE_THETA = 10000.0
ATTN_W = N_Q_HEADS * HEAD_DIM
KV_W = N_KV_HEADS * HEAD_DIM
HGRN_HEADS = 16
HGRN_DK = 128
HGRN_DV = 128
HGRN_FW = HGRN_HEADS * HGRN_DK
HGRN_IW = HGRN_HEADS * HGRN_DV
CONV_WIDTH = 3
NORM_EPS = 1e-5
MASK_VALUE = -1e30
MIN_FORGET = 1e-30
MAX_LB = 0.999
LOG2_E = 1.4426950408889634

COL_Q = 0
COL_K = COL_Q + ATTN_W
COL_V = COL_K + KV_W
COL_HQ = COL_V + KV_W
COL_HF = COL_HQ + HGRN_FW
COL_HI = COL_HF + HGRN_FW
COL_HG = COL_HI + HGRN_IW
COL_GA = COL_HG + HGRN_IW

LANES = 128
BF16_ROWS = 16
VMEM_LIMIT = 56 * 1024 * 1024
HGRN_CHUNK = 64
HGRN_BLOCK = 8
HGRN_SLAB = 4
HGRN_OUT_SLAB = 16
NORM_ROWS = 64

f32 = jnp.float32
bf16 = jnp.bfloat16


def _params(sem):
    return pltpu.CompilerParams(dimension_semantics=sem, vmem_limit_bytes=VMEM_LIMIT)


def _norm_rows_into(x_ref, g_ref, h_ref, n_rows, dst_off):
    step = min(NORM_ROWS, n_rows)
    d = x_ref.shape[1]

    def body(c, carry):
        r0 = pl.multiple_of(c * step, step)
        rows = pl.ds(r0, step)
        ss = jnp.zeros((step, LANES), f32)
        for lc in range(d // LANES):
            xc = x_ref[rows, lc * LANES:(lc + 1) * LANES]
            ss = ss + xc * xc
        scale = lax.rsqrt(jnp.sum(ss, axis=-1, keepdims=True) * (1.0 / d) + NORM_EPS)
        for lc in range(d // LANES):
            lanes = slice(lc * LANES, (lc + 1) * LANES)
            y = x_ref[rows, lanes] * scale * g_ref[:, lanes]
            h_ref[pl.ds(dst_off + r0, step), lanes] = y.astype(h_ref.dtype)
        return carry

    lax.fori_loop(0, n_rows // step, body, 0)


def _proj_body(x_ref, g_ref, w_ref, cos_ref, sin_ref, o_ref, h_ref):
    j = pl.program_id(1)
    bn = o_ref.shape[1]

    @pl.when(j == 0)
    def _():
        _norm_rows_into(x_ref, g_ref, h_ref, x_ref.shape[0], 0)

    w = w_ref[...].astype(bf16)
    acc = jnp.dot(h_ref[...], w, preferred_element_type=f32)

    rope_blocks = -(-COL_V // bn)

    @pl.when(j < rope_blocks)
    def _():
        half = HEAD_DIM // 2
        first_half = (lax.broadcasted_iota(jnp.int32, cos_ref.shape, 1) % HEAD_DIM) < half
        for c in range(bn // LANES):
            col = j * bn + c * LANES
            scale = jnp.where(col < COL_K, HEAD_DIM ** -0.5, 1.0)
            rotate = col < COL_V
            cos = jnp.where(rotate, cos_ref[...] * scale, 1.0)
            sin = jnp.where(rotate, sin_ref[...] * scale, 0.0)
            t = acc[:, c * LANES:(c + 1) * LANES]
            fwd = pltpu.roll(t, LANES - half, axis=1)
            bwd = pltpu.roll(t, half, axis=1)
            o_ref[:, c * LANES:(c + 1) * LANES] = (
                t * cos + jnp.where(first_half, fwd, bwd) * sin).astype(o_ref.dtype)

    @pl.when(j >= rope_blocks)
    def _():
        o_ref[...] = acc.astype(o_ref.dtype)


def _norm_proj(x, g, w, cos, sin, layer, bm, bn):
    m, d = x.shape
    n = w.shape[2]
    return pl.pallas_call(
        _proj_body,
        grid=(m // bm, n // bn),
        in_specs=[
            pl.BlockSpec((bm, d), lambda i, j: (i, 0), pipeline_mode=pl.Buffered(1)),
            pl.BlockSpec((1, d), lambda i, j: (0, 0)),
            pl.BlockSpec((None, d, bn), lambda i, j: (layer, 0, j)),
            pl.BlockSpec((bm, LANES), lambda i, j: (i, 0)),
            pl.BlockSpec((bm, LANES), lambda i, j: (i, 0)),
        ],
        out_specs=pl.BlockSpec((bm, bn), lambda i, j: (i, j)),
        out_shape=jax.ShapeDtypeStruct((m, n), bf16),
        scratch_shapes=[pltpu.VMEM((bm, d), bf16)],
        compiler_params=_params(("parallel", "arbitrary")),
        name="norm_proj",
    )(x, g.reshape(1, d), w, cos, sin)


def _rope_body(pos_ref, inv_ref, cos_ref, sin_ref):
    ang = pos_ref[...].astype(f32) * inv_ref[...]
    lane = lax.broadcasted_iota(jnp.int32, ang.shape, 1)
    first_half = (lane % HEAD_DIM) < (HEAD_DIM // 2)
    cos_ref[...] = jnp.cos(ang)
    s = jnp.sin(ang)
    sin_ref[...] = jnp.where(first_half, -s, s)


def _rope_tables(positions, bs):
    s = positions.shape[-1]
    half = HEAD_DIM // 2
    inv_freq = ROPE_THETA ** (-jnp.arange(0, HEAD_DIM, 2, dtype=f32) / HEAD_DIM)
    inv_row = jnp.tile(inv_freq, LANES // half).reshape(1, LANES)
    return pl.pallas_call(
        _rope_body,
        grid=(s // bs,),
        in_specs=[pl.BlockSpec((bs, 1), lambda i: (i, 0)),
                  pl.BlockSpec((1, LANES), lambda i: (0, 0))],
        out_specs=[pl.BlockSpec((bs, LANES), lambda i: (i, 0))] * 2,
        out_shape=[jax.ShapeDtypeStruct((s, LANES), f32)] * 2,
        compiler_params=_params(("parallel",)),
        name="rope_tables",
    )(positions.reshape(s, 1), inv_row)


def _dup_head(pair, odd, low_lanes):
    swapped = pltpu.roll(pair, HEAD_DIM, axis=1)
    take_own = low_lanes != odd
    return jnp.where(take_own, pair, swapped)


def _attn_body(sink_ref, q_ref, kc_ref, kp_ref, vc_ref, vp_ref, o_ref, qs_ref, lg_ref, p_ref):
    n = pl.program_id(0)
    w = WINDOW
    pairs = Q_GROUP // 2
    low = lax.broadcasted_iota(jnp.int32, (w, LANES), 1) < HEAD_DIM
    low2 = lax.broadcasted_iota(jnp.int32, (2 * w, LANES), 1) < HEAD_DIM
    key_row = lax.broadcasted_iota(jnp.int32, (2 * w, LANES), 0)

    qi = lax.broadcasted_iota(jnp.int32, (w, 2 * w), 0)
    sj = lax.broadcasted_iota(jnp.int32, (w, 2 * w), 1)
    dist = qi + w - sj
    mask = (dist >= 0) & (dist < w) & ((n > 0) | (sj >= w))
    sink_col = lax.broadcasted_iota(jnp.int32, (1, 2 * w), 1) == 0

    for kg in range(N_KV_HEADS):
        c = kg // 2
        odd = (kg % 2) == 1
        cols = slice(c * LANES, (c + 1) * LANES)
        kband = jnp.concatenate([kp_ref[:, cols], kc_ref[:, cols]], axis=0).astype(f32)
        kk = _dup_head(kband, odd, low2).astype(bf16)
        vband = jnp.concatenate([vp_ref[:, cols], vc_ref[:, cols]], axis=0).astype(f32)
        vv = jnp.where(key_row == 0, 0.0, _dup_head(vband, odd, low2))
        ones = jnp.ones((2 * w, LANES), f32)
        v_even = jnp.concatenate([jnp.where(low2, vv, 0.0), jnp.where(low2, ones, 0.0)], axis=1).astype(bf16)
        v_odd = jnp.concatenate([jnp.where(low2, 0.0, vv), jnp.where(low2, 0.0, ones)], axis=1).astype(bf16)

        for p in range(pairs):
            col = kg * Q_GROUP * HEAD_DIM + p * LANES
            qp = q_ref[:, col:col + LANES]
            zero = jnp.zeros_like(qp)
            qs_ref[p * w:(p + 1) * w, :] = jnp.where(low, qp, zero)
            qs_ref[(pairs + p) * w:(pairs + p + 1) * w, :] = jnp.where(low, zero, qp)

        lg_ref[...] = lax.dot_general(qs_ref[...], kk, (((1,), (1,)), ((), ())),
                                      preferred_element_type=f32)
        for e in range(2):
            for p in range(pairs):
                slab = e * pairs + p
                rows = slice(slab * w, (slab + 1) * w)
                fill = jnp.where(sink_col, sink_ref[kg * Q_GROUP + 2 * p + e], MASK_VALUE)
                lg = jnp.where(mask, lg_ref[rows, :], fill)
                m = jnp.max(lg, axis=-1, keepdims=True)
                p_ref[rows, :] = jnp.exp(lg - m).astype(bf16)
        half_rows = pairs * w
        r = (jnp.dot(p_ref[0:half_rows, :], v_even, preferred_element_type=f32)
             + jnp.dot(p_ref[half_rows:, :], v_odd, preferred_element_type=f32))
        o = (r[:, 0:LANES] / r[:, LANES:]).astype(o_ref.dtype)
        for p in range(pairs):
            col = kg * Q_GROUP * HEAD_DIM + p * LANES
            o_ref[:, col:col + LANES] = o[p * w:(p + 1) * w, :]


def _attention(proj, sinks):
    s = proj.shape[0]
    w = WINDOW
    nb = s // w
    kcol, vcol = COL_K // KV_W, COL_V // KV_W
    grid_spec = pltpu.PrefetchScalarGridSpec(
        num_scalar_prefetch=1,
        grid=(nb,),
        in_specs=[
            pl.BlockSpec((w, ATTN_W), lambda n, sk: (n, 0)),
            pl.BlockSpec((w, KV_W), lambda n, sk: (n, kcol)),
            pl.BlockSpec((w, KV_W), lambda n, sk: (jnp.maximum(n - 1, 0), kcol)),
            pl.BlockSpec((w, KV_W), lambda n, sk: (n, vcol)),
            pl.BlockSpec((w, KV_W), lambda n, sk: (jnp.maximum(n - 1, 0), vcol)),
        ],
        out_specs=pl.BlockSpec((w, ATTN_W), lambda n, sk: (n, 0)),
        scratch_shapes=[
            pltpu.VMEM((Q_GROUP * w, LANES), bf16),
            pltpu.VMEM((Q_GROUP * w, 2 * w), f32),
            pltpu.VMEM((Q_GROUP * w, 2 * w), bf16),
        ],
    )
    return pl.pallas_call(
        _attn_body,
        grid_spec=grid_spec,
        out_shape=jax.ShapeDtypeStruct((s, ATTN_W), bf16),
        compiler_params=_params(("parallel",)),
        name="swa_attention",
    )(sinks.astype(f32), proj, proj, proj, proj, proj)


def _cumsum_chunks(g, row_in_chunk):
    b = g
    shift = 1
    while shift < HGRN_CHUNK:
        b = b + jnp.where(row_in_chunk >= shift, pltpu.roll(b, shift, axis=0), 0.0)
        shift *= 2
    return b


def _hgrn_body(q_ref, z_ref, i_ref, og_ref, lbp_ref, ng_ref, o_ref,
               st_ref, qd_ref, stb_ref, upd_ref, dec_ref, sc_ref, bk_ref, *, layer):
    c = HGRN_CHUNK
    hb = HGRN_BLOCK
    nblk = c // hb
    ga = HGRN_SLAB
    bt = q_ref.shape[0]
    n_chunks = bt // c
    slab = ga * c
    nb3 = slab // hb

    @pl.when(pl.program_id(1) == 0)
    def _():
        st_ref[...] = jnp.zeros_like(st_ref)

    lbp = lbp_ref[...]
    e = jnp.exp(lbp - jnp.max(lbp, axis=0, keepdims=True))
    sm = e / jnp.sum(e, axis=0, keepdims=True)
    cum = sm[0:1, :]
    for l in range(1, layer + 1):
        cum = cum + sm[l:l + 1, :]
    lower = jnp.clip(cum - sm[0:1, :], 0.0, MAX_LB)
    qscale = HGRN_DK ** -0.5
    row_in_chunk = lax.broadcasted_iota(jnp.int32, (slab, LANES), 0) % c
    blk_id = lax.broadcasted_iota(jnp.int32, (nb3, hb, LANES), 0) % nblk
    col = lax.broadcasted_iota(jnp.int32, (nb3, hb, LANES), 2) - hb * blk_id
    tr = lax.broadcasted_iota(jnp.int32, (ga, c, c), 1)
    ts = lax.broadcasted_iota(jnp.int32, (ga, c, c), 2)
    causal = tr >= ts

    def phase_a(si, carry):
        r0 = pl.multiple_of(si * slab, slab)
        rows = pl.ds(r0, slab)
        q = q_ref[rows, :].astype(f32) * qscale
        z = z_ref[rows, :].astype(f32)
        f = lower + (1.0 - lower) * jax.nn.sigmoid(z)
        g = jnp.log(jnp.maximum(f, MIN_FORGET))
        k = 1.0 - f
        b = _cumsum_chunks(g, row_in_chunk) * LOG2_E
        bc = b.reshape(ga, c, LANES)
        b_last = bc[:, c - 1:c, :]
        b_last_r = jnp.broadcast_to(b_last, (ga, c, LANES)).reshape(slab, LANES)
        vb = i_ref[rows, :]

        qd_ref[rows, :] = (q * jnp.exp2(b)).astype(bf16)
        kd = (k * jnp.exp2(b_last_r - b)).astype(bf16)

        q3 = q.reshape(nb3, hb, LANES)
        b3 = b.reshape(nb3, hb, LANES)
        bk_ref[0] = b3
        bk_ref[1] = k.reshape(nb3, hb, LANES)
        a_diag = jnp.zeros((nb3, hb, LANES), f32)
        for s in range(hb):
            w = jnp.exp2(b3 - bk_ref[0, :, s:s + 1, :])
            a = jnp.sum(q3 * bk_ref[1, :, s:s + 1, :] * w, axis=-1, keepdims=True)
            a_diag = jnp.where(col == s, a, a_diag)
        scores = a_diag.reshape(ga, c, LANES)[:, :, 0:c]

        group = 2 * hb
        while group <= c:
            n_groups = slab // group
            half = group // 2
            b_mid = b.reshape(n_groups, group, LANES)[:, half - 1:half, :]
            b_mid_r = jnp.broadcast_to(b_mid, (n_groups, group, LANES)).reshape(slab, LANES)
            upper = (row_in_chunk % group) >= half
            qt = jnp.where(upper, q * jnp.exp2(b - b_mid_r), 0.0).astype(bf16).reshape(ga, c, LANES)
            kt = jnp.where(upper, 0.0, k * jnp.exp2(b_mid_r - b)).astype(bf16).reshape(ga, c, LANES)
            a_off = jnp.einsum("gtk,gsk->gts", qt, kt, preferred_element_type=f32)
            if group < c:
                a_off = jnp.where((tr // group) == (ts // group), a_off, 0.0)
            scores = scores + a_off
            group *= 2
        sc_ref[rows, :] = jnp.where(causal, scores, 0.0).astype(bf16).reshape(slab, c)

        for gi in range(ga):
            upd_ref[si * ga + gi] = lax.dot_general(
                vb[gi * c:(gi + 1) * c, :], kd[gi * c:(gi + 1) * c, :],
                (((0,), (0,)), ((), ())), preferred_element_type=f32)
        dec_ref[pl.ds(si * ga, ga), :, :] = jnp.exp2(b_last)
        return carry

    lax.fori_loop(0, bt // slab, phase_a, 0)

    def phase_b(ci, st):
        stb_ref[ci] = st.astype(bf16)
        return st * dec_ref[ci] + upd_ref[ci]

    st_ref[...] = lax.fori_loop(0, n_chunks, phase_b, st_ref[...], unroll=4)

    ng = ng_ref[...]
    gc = HGRN_OUT_SLAB
    oslab = gc * c

    def phase_c(si, carry):
        r0 = pl.multiple_of(si * oslab, oslab)
        rows = pl.ds(r0, oslab)
        outs = []
        for gi in range(gc):
            cr = pl.ds(r0 + gi * c, c)
            inter = lax.dot_general(qd_ref[cr, :], stb_ref[si * gc + gi],
                                    (((1,), (1,)), ((), ())), preferred_element_type=f32)
            outs.append(inter + jnp.dot(sc_ref[cr, :], i_ref[cr, :], preferred_element_type=f32))
        o = jnp.concatenate(outs, axis=0)
        o = o * lax.rsqrt(jnp.mean(o * o, axis=-1, keepdims=True) + NORM_EPS) * ng
        og = og_ref[rows, :].astype(f32)
        o_ref[rows, :] = (o * (og * jax.nn.sigmoid(og))).astype(o_ref.dtype)
        return carry

    lax.fori_loop(0, bt // oslab, phase_c, 0)


def _hgrn(proj, lb_param, norm_g, layer, bt):
    s = proj.shape[0]
    depth = lb_param.shape[0]
    n_chunks = bt // HGRN_CHUNK
    cq, cf, ci, cg = (COL_HQ // LANES, COL_HF // LANES, COL_HI // LANES, COL_HG // LANES)
    return pl.pallas_call(
        functools.partial(_hgrn_body, layer=layer),
        grid=(HGRN_HEADS, s // bt),
        in_specs=[
            pl.BlockSpec((bt, LANES), lambda h, t: (t, cq + h)),
            pl.BlockSpec((bt, LANES), lambda h, t: (t, cf + h)),
            pl.BlockSpec((bt, LANES), lambda h, t: (t, ci + h)),
            pl.BlockSpec((bt, LANES), lambda h, t: (t, cg + h)),
            pl.BlockSpec((depth, LANES), lambda h, t: (0, h)),
            pl.BlockSpec((1, LANES), lambda h, t: (0, h)),
        ],
        out_specs=pl.BlockSpec((bt, LANES), lambda h, t: (t, h)),
        out_shape=jax.ShapeDtypeStruct((s, HGRN_IW), bf16),
        scratch_shapes=[
            pltpu.VMEM((HGRN_DV, HGRN_DK), f32),
            pltpu.VMEM((bt, LANES), bf16),
            pltpu.VMEM((n_chunks, HGRN_DV, HGRN_DK), bf16),
            pltpu.VMEM((n_chunks, HGRN_DV, HGRN_DK), f32),
            pltpu.VMEM((n_chunks, 1, LANES), f32),
            pltpu.VMEM((bt, HGRN_CHUNK), bf16),
            pltpu.VMEM((2, HGRN_SLAB * HGRN_CHUNK // HGRN_BLOCK, HGRN_BLOCK, LANES), f32),
        ],
        compiler_params=_params(("parallel", "arbitrary")),
        name="hgrn2",
    )(proj, proj, proj, proj, lb_param.astype(f32), norm_g.reshape(1, -1).astype(f32))


def _merge_body(a_ref, b_ref, pa_ref, ph_ref, ga_ref, gb_ref, o_ref):
    ya = jnp.dot(a_ref[...], pa_ref[...].astype(bf16), preferred_element_type=f32)
    yb = jnp.dot(b_ref[...], ph_ref[...].astype(bf16), preferred_element_type=f32)
    ga = jax.nn.sigmoid(ga_ref[...].astype(f32))
    gb = jax.nn.sigmoid(gb_ref[...].astype(f32))
    o_ref[...] = (ga * ya + gb * yb).astype(o_ref.dtype)


def _merge(a, b, pa, ph, proj, layer, bm, bn):
    m, ka = a.shape
    kb = b.shape[1]
    d = pa.shape[2]
    ca = COL_GA // bn
    cb = (COL_GA + d) // bn
    return pl.pallas_call(
        _merge_body,
        grid=(m // bm, d // bn),
        in_specs=[
            pl.BlockSpec((bm, ka), lambda i, j: (i, 0)),
            pl.BlockSpec((bm, kb), lambda i, j: (i, 0)),
            pl.BlockSpec((None, ka, bn), lambda i, j: (layer, 0, j)),
            pl.BlockSpec((None, kb, bn), lambda i, j: (layer, 0, j)),
            pl.BlockSpec((bm, bn), lambda i, j: (i, ca + j)),
            pl.BlockSpec((bm, bn), lambda i, j: (i, cb + j)),
        ],
        out_specs=pl.BlockSpec((bm, bn), lambda i, j: (i, j)),
        out_shape=jax.ShapeDtypeStruct((m, d), bf16),
        compiler_params=_params(("parallel", "arbitrary")),
        name="gated_merge",
    )(a, b, pa, ph, proj, proj)


def _resid_body(a_ref, w_ref, x_ref, o_ref):
    o_ref[...] = x_ref[...] + jnp.dot(a_ref[...], w_ref[...].astype(bf16), preferred_element_type=f32)


def _resid_matmul(a, w, x, layer, bm, bn, in_place):
    m, k = a.shape
    d = w.shape[2]
    return pl.pallas_call(
        _resid_body,
        grid=(m // bm, d // bn),
        in_specs=[
            pl.BlockSpec((bm, k), lambda i, j: (i, 0)),
            pl.BlockSpec((None, k, bn), lambda i, j: (layer, 0, j)),
            pl.BlockSpec((bm, bn), lambda i, j: (i, j)),
        ],
        out_specs=pl.BlockSpec((bm, bn), lambda i, j: (i, j)),
        out_shape=jax.ShapeDtypeStruct((m, d), f32),
        input_output_aliases={2: 0} if in_place else {},
        compiler_params=_params(("parallel", "arbitrary")),
        name="resid_matmul",
    )(a, w, x)


def _ffn_in_body(x_ref, xh_ref, g_ref, wu_ref, wg_ref, cw_ref, cb_ref, o_ref, h_ref, u_ref):
    bm = x_ref.shape[0]
    halo = BF16_ROWS

    @pl.when(pl.program_id(1) == 0)
    def _():
        _norm_rows_into(x_ref, g_ref, h_ref, bm, halo)

        @pl.when(pl.program_id(0) == 0)
        def _():
            h_ref[0:halo, :] = jnp.zeros((halo, h_ref.shape[1]), h_ref.dtype)

        @pl.when(pl.program_id(0) > 0)
        def _():
            _norm_rows_into(xh_ref, g_ref, h_ref, halo, 0)

    u_ref[...] = jnp.dot(h_ref[...], wu_ref[...].astype(bf16), preferred_element_type=f32)
    gate = jnp.dot(h_ref[halo:, :], wg_ref[...].astype(bf16), preferred_element_type=f32)
    cw = cw_ref[...]
    u = cb_ref[...] + cw[CONV_WIDTH - 1:CONV_WIDTH, :] * u_ref[halo:, :]
    for j in range(CONV_WIDTH - 1):
        back = CONV_WIDTH - 1 - j
        u = u + cw[j:j + 1, :] * u_ref[pl.ds(halo - back, bm), :]
    gelu = 0.5 * u * (1.0 + lax.erf(u * (2.0 ** -0.5)))
    o_ref[...] = (gelu * gate).astype(o_ref.dtype)


def _ffn_in(x, g, w, conv_w, conv_b, layer, bm, bn):
    m, d = x.shape
    ff = conv_w.shape[2]
    halo = BF16_ROWS
    nj = ff // bn
    return pl.pallas_call(
        _ffn_in_body,
        grid=(m // bm, nj),
        in_specs=[
            pl.BlockSpec((bm, d), lambda i, j: (i, 0), pipeline_mode=pl.Buffered(1)),
            pl.BlockSpec((halo, d), lambda i, j: (jnp.maximum(i * (bm // halo) - 1, 0), 0)),
            pl.BlockSpec((1, d), lambda i, j: (0, 0)),
            pl.BlockSpec((None, d, bn), lambda i, j: (layer, 0, j)),
            pl.BlockSpec((None, d, bn), lambda i, j: (layer, 0, nj + j)),
            pl.BlockSpec((None, CONV_WIDTH, bn), lambda i, j: (layer, 0, j)),
            pl.BlockSpec((1, bn), lambda i, j: (0, j)),
        ],
        out_specs=pl.BlockSpec((bm, bn), lambda i, j: (i, j)),
        out_shape=jax.ShapeDtypeStruct((m, ff), bf16),
        scratch_shapes=[pltpu.VMEM((bm + halo, d), bf16), pltpu.VMEM((bm + halo, bn), f32)],
        compiler_params=_params(("parallel", "arbitrary")),
        name="ffn_in",
    )(x, x, g.reshape(1, d), w, w, conv_w, conv_b.reshape(1, ff))


def _final_norm_body(x_ref, g_ref, o_ref):
    x = x_ref[...]
    ms = jnp.mean(x * x, axis=-1, keepdims=True)
    o_ref[...] = x * lax.rsqrt(ms + NORM_EPS) * g_ref[...]


def _final_norm(x, g, bm):
    m, d = x.shape
    return pl.pallas_call(
        _final_norm_body,
        grid=(m // bm,),
        in_specs=[pl.BlockSpec((bm, d), lambda i: (i, 0)), pl.BlockSpec((1, d), lambda i: (0, 0))],
        out_specs=pl.BlockSpec((bm, d), lambda i: (i, 0)),
        out_shape=jax.ShapeDtypeStruct((m, d), f32),
        compiler_params=_params(("parallel",)),
        name="final_norm",
    )(x, g.reshape(1, d))


def _tiles(seq):
    bm = min(1024, seq)
    return dict(bm=bm, bn=512, bn_ffn=256, bn_deep=256, bt=min(1024, seq),
                rope_rows=min(512, seq), norm_rows=min(256, seq))


def kernel(x, positions, norm1_g, w_in, attn_sinks, lb_param, hgrn_norm_g, p_attn, p_hgrn, w_out,
           norm2_g, w_ffn_in, conv_w, conv_b, w_down, final_norm_g):
    batch, seq, d = x.shape
    depth = w_in.shape[0]
    t = _tiles(seq)
    bm, bn = t["bm"], t["bn"]
    w_down = w_down.astype(bf16)
    outs = []
    for bi in range(batch):
        xs = x[bi]
        cos, sin = _rope_tables(positions[bi], t["rope_rows"])
        for l in range(depth):
            proj = _norm_proj(xs, norm1_g[l], w_in, cos, sin, l, bm, bn)
            a_out = _attention(proj, attn_sinks[l])
            b_out = _hgrn(proj, lb_param, hgrn_norm_g[l], l, t["bt"])
            merged = _merge(a_out, b_out, p_attn, p_hgrn, proj, l, bm, bn)
            xs = _resid_matmul(merged, w_out, xs, l, bm, bn, in_place=l > 0)
            act = _ffn_in(xs, norm2_g[l], w_ffn_in, conv_w, conv_b[l], l, bm, t["bn_ffn"])
            xs = _resid_matmul(act, w_down, xs, l, bm, t["bn_deep"], in_place=True)
        outs.append(_final_norm(xs, final_norm_g, t["norm_rows"]))
    return jnp.stack(outs, axis=0)
```

```python
import functools

import jax
import jax.numpy as jnp
from jax import lax
from jax.experimental import pallas as pl
from jax.experimental.pallas import tpu as pltpu

N_Q_HEADS = 32
N_KV_HEADS = 4
HEAD_DIM = 64
Q_GROUP = N_Q_HEADS // N_KV_HEADS
WINDOW = 128
ROPE_THETA = 10000.0
ATTN_W = N_Q_HEADS * HEAD_DIM
KV_W = N_KV_HEADS * HEAD_DIM
HGRN_HEADS = 16
HGRN_DK = 128
HGRN_DV = 128
HGRN_FW = HGRN_HEADS * HGRN_DK
HGRN_IW = HGRN_HEADS * HGRN_DV
CONV_WIDTH = 3
NORM_EPS = 1e-5
MASK_VALUE = -1e30
MIN_FORGET = 1e-30
MAX_LB = 0.999
LOG2_E = 1.4426950408889634

COL_Q = 0
COL_K = COL_Q + ATTN_W
COL_V = COL_K + KV_W
COL_HQ = COL_V + KV_W
COL_HF = COL_HQ + HGRN_FW
COL_HI = COL_HF + HGRN_FW
COL_HG = COL_HI + HGRN_IW
COL_GA = COL_HG + HGRN_IW

LANES = 128
BF16_ROWS = 16
VMEM_LIMIT = 56 * 1024 * 1024
HGRN_CHUNK = 64
HGRN_BLOCK = 8
HGRN_SLAB = 4
HGRN_OUT_SLAB = 16
NORM_ROWS = 64

f32 = jnp.float32
bf16 = jnp.bfloat16


def _params(sem):
    return pltpu.CompilerParams(dimension_semantics=sem, vmem_limit_bytes=VMEM_LIMIT)


def _norm_rows_into(x_ref, g_ref, h_ref, n_rows, dst_off):
    step = min(NORM_ROWS, n_rows)
    d = x_ref.shape[1]

    def body(c, carry):
        r0 = pl.multiple_of(c * step, step)
        rows = pl.ds(r0, step)
        ss = jnp.zeros((step, LANES), f32)
        for lc in range(d // LANES):
            xc = x_ref[rows, lc * LANES:(lc + 1) * LANES]
            ss = ss + xc * xc
        scale = lax.rsqrt(jnp.sum(ss, axis=-1, keepdims=True) * (1.0 / d) + NORM_EPS)
        for lc in range(d // LANES):
            lanes = slice(lc * LANES, (lc + 1) * LANES)
            y = x_ref[rows, lanes] * scale * g_ref[:, lanes]
            h_ref[pl.ds(dst_off + r0, step), lanes] = y.astype(h_ref.dtype)
        return carry

    lax.fori_loop(0, n_rows // step, body, 0)


def _proj_body(x_ref, g_ref, w_ref, cos_ref, sin_ref, o_ref, h_ref):
    j = pl.program_id(1)
    bn = o_ref.shape[1]

    @pl.when(j == 0)
    def _():
        _norm_rows_into(x_ref, g_ref, h_ref, x_ref.shape[0], 0)

    def matmul():
        w = w_ref[...].astype(bf16)
        return jnp.dot(h_ref[...], w, preferred_element_type=f32)

    rope_blocks = -(-COL_V // bn)

    @pl.when(j < rope_blocks)
    def _():
        acc = matmul()
        half = HEAD_DIM // 2
        first_half = (lax.broadcasted_iota(jnp.int32, cos_ref.shape, 1) % HEAD_DIM) < half
        for c in range(bn // LANES):
            col = j * bn + c * LANES
            scale = jnp.where(col < COL_K, HEAD_DIM ** -0.5, 1.0)
            rotate = col < COL_V
            cos = jnp.where(rotate, cos_ref[...] * scale, 1.0)
            sin = jnp.where(rotate, sin_ref[...] * scale, 0.0)
            t = acc[:, c * LANES:(c + 1) * LANES]
            fwd = pltpu.roll(t, LANES - half, axis=1)
            bwd = pltpu.roll(t, half, axis=1)
            o_ref[:, c * LANES:(c + 1) * LANES] = (
                t * cos + jnp.where(first_half, fwd, bwd) * sin).astype(o_ref.dtype)

    @pl.when(j >= rope_blocks)
    def _():
        o_ref[...] = matmul().astype(o_ref.dtype)


def _norm_proj(x, g, w, cos, sin, layer, bm, bn):
    m, d = x.shape
    n = w.shape[2]
    return pl.pallas_call(
        _proj_body,
        grid=(m // bm, n // bn),
        in_specs=[
            pl.BlockSpec((bm, d), lambda i, j: (i, 0), pipeline_mode=pl.Buffered(1)),
            pl.BlockSpec((1, d), lambda i, j: (0, 0)),
            pl.BlockSpec((None, d, bn), lambda i, j: (layer, 0, j)),
            pl.BlockSpec((bm, LANES), lambda i, j: (i, 0)),
            pl.BlockSpec((bm, LANES), lambda i, j: (i, 0)),
        ],
        out_specs=pl.BlockSpec((bm, bn), lambda i, j: (i, j)),
        out_shape=jax.ShapeDtypeStruct((m, n), bf16),
        scratch_shapes=[pltpu.VMEM((bm, d), bf16)],
        compiler_params=_params(("parallel", "arbitrary")),
        name="norm_proj",
    )(x, g.reshape(1, d), w, cos, sin)


def _rope_body(pos_ref, inv_ref, cos_ref, sin_ref):
    ang = pos_ref[...].astype(f32) * inv_ref[...]
    lane = lax.broadcasted_iota(jnp.int32, ang.shape, 1)
    first_half = (lane % HEAD_DIM) < (HEAD_DIM // 2)
    cos_ref[...] = jnp.cos(ang)
    s = jnp.sin(ang)
    sin_ref[...] = jnp.where(first_half, -s, s)


def _rope_tables(positions, bs):
    s = positions.shape[-1]
    half = HEAD_DIM // 2
    inv_freq = ROPE_THETA ** (-jnp.arange(0, HEAD_DIM, 2, dtype=f32) / HEAD_DIM)
    inv_row = jnp.tile(inv_freq, LANES // half).reshape(1, LANES)
    return pl.pallas_call(
        _rope_body,
        grid=(s // bs,),
        in_specs=[pl.BlockSpec((bs, 1), lambda i: (i, 0)),
                  pl.BlockSpec((1, LANES), lambda i: (0, 0))],
        out_specs=[pl.BlockSpec((bs, LANES), lambda i: (i, 0))] * 2,
        out_shape=[jax.ShapeDtypeStruct((s, LANES), f32)] * 2,
        compiler_params=_params(("parallel",)),
        name="rope_tables",
    )(positions.reshape(s, 1), inv_row)


def _dup_head(pair, odd, low_lanes):
    swapped = pltpu.roll(pair, HEAD_DIM, axis=1)
    take_own = low_lanes != odd
    return jnp.where(take_own, pair, swapped)


def _attn_body(sink_ref, q_ref, kc_ref, kp_ref, vc_ref, vp_ref, o_ref, qs_ref, lg_ref, p_ref):
    n = pl.program_id(0)
    w = WINDOW
    pairs = Q_GROUP // 2
    low = lax.broadcasted_iota(jnp.int32, (w, LANES), 1) < HEAD_DIM
    low2 = lax.broadcasted_iota(jnp.int32, (2 * w, LANES), 1) < HEAD_DIM
    key_row = lax.broadcasted_iota(jnp.int32, (2 * w, LANES), 0)

    qi = lax.broadcasted_iota(jnp.int32, (w, 2 * w), 0)
    sj = lax.broadcasted_iota(jnp.int32, (w, 2 * w), 1)
    dist = qi + w - sj
    mask = (dist >= 0) & (dist < w) & ((n > 0) | (sj >= w))
    sink_col = lax.broadcasted_iota(jnp.int32, (1, 2 * w), 1) == 0

    for kg in range(N_KV_HEADS):
        c = kg // 2
        odd = (kg % 2) == 1
        cols = slice(c * LANES, (c + 1) * LANES)
        kband = jnp.concatenate([kp_ref[:, cols], kc_ref[:, cols]], axis=0).astype(f32)
        kk = _dup_head(kband, odd, low2).astype(bf16)
        vband = jnp.concatenate([vp_ref[:, cols], vc_ref[:, cols]], axis=0).astype(f32)
        vv = jnp.where(key_row == 0, 0.0, _dup_head(vband, odd, low2))
        ones = jnp.ones((2 * w, LANES), f32)
        v_even = jnp.concatenate([jnp.where(low2, vv, 0.0), jnp.where(low2, ones, 0.0)], axis=1).astype(bf16)
        v_odd = jnp.concatenate([jnp.where(low2, 0.0, vv), jnp.where(low2, 0.0, ones)], axis=1).astype(bf16)

        for p in range(pairs):
            col = kg * Q_GROUP * HEAD_DIM + p * LANES
            qp = q_ref[:, col:col + LANES]
            zero = jnp.zeros_like(qp)
            qs_ref[p * w:(p + 1) * w, :] = jnp.where(low, qp, zero)
            qs_ref[(pairs + p) * w:(pairs + p + 1) * w, :] = jnp.where(low, zero, qp)

        lg_ref[...] = lax.dot_general(qs_ref[...], kk, (((1,), (1,)), ((), ())),
                                      preferred_element_type=f32)
        for e in range(2):
            for p in range(pairs):
                slab = e * pairs + p
                rows = slice(slab * w, (slab + 1) * w)
                fill = jnp.where(sink_col, sink_ref[kg * Q_GROUP + 2 * p + e], MASK_VALUE)
                lg = jnp.where(mask, lg_ref[rows, :], fill)
                m = jnp.max(lg, axis=-1, keepdims=True)
                p_ref[rows, :] = jnp.exp(lg - m).astype(bf16)
        half_rows = pairs * w
        r = (jnp.dot(p_ref[0:half_rows, :], v_even, preferred_element_type=f32)
             + jnp.dot(p_ref[half_rows:, :], v_odd, preferred_element_type=f32))
        o = (r[:, 0:LANES] / r[:, LANES:]).astype(o_ref.dtype)
        for p in range(pairs):
            col = kg * Q_GROUP * HEAD_DIM + p * LANES
            o_ref[:, col:col + LANES] = o[p * w:(p + 1) * w, :]


def _attention(proj, sinks):
    s = proj.shape[0]
    w = WINDOW
    nb = s // w
    kcol, vcol = COL_K // KV_W, COL_V // KV_W
    grid_spec = pltpu.PrefetchScalarGridSpec(
        num_scalar_prefetch=1,
        grid=(nb,),
        in_specs=[
            pl.BlockSpec((w, ATTN_W), lambda n, sk: (n, 0)),
            pl.BlockSpec((w, KV_W), lambda n, sk: (n, kcol)),
            pl.BlockSpec((w, KV_W), lambda n, sk: (jnp.maximum(n - 1, 0), kcol)),
            pl.BlockSpec((w, KV_W), lambda n, sk: (n, vcol)),
            pl.BlockSpec((w, KV_W), lambda n, sk: (jnp.maximum(n - 1, 0), vcol)),
        ],
        out_specs=pl.BlockSpec((w, ATTN_W), lambda n, sk: (n, 0)),
        scratch_shapes=[
            pltpu.VMEM((Q_GROUP * w, LANES), bf16),
            pltpu.VMEM((Q_GROUP * w, 2 * w), f32),
            pltpu.VMEM((Q_GROUP * w, 2 * w), bf16),
        ],
    )
    return pl.pallas_call(
        _attn_body,
        grid_spec=grid_spec,
        out_shape=jax.ShapeDtypeStruct((s, ATTN_W), bf16),
        compiler_params=_params(("parallel",)),
        name="swa_attention",
    )(sinks.astype(f32), proj, proj, proj, proj, proj)


def _cumsum_chunks(g, row_in_chunk):
    b = g
    shift = 1
    while shift < HGRN_CHUNK:
        b = b + jnp.where(row_in_chunk >= shift, pltpu.roll(b, shift, axis=0), 0.0)
        shift *= 2
    return b


def _hgrn_body(q_ref, z_ref, i_ref, og_ref, lbp_ref, ng_ref, o_ref,
               st_ref, qd_ref, stb_ref, upd_ref, dec_ref, sc_ref, bk_ref, *, layer):
    c = HGRN_CHUNK
    hb = HGRN_BLOCK
    nblk = c // hb
    ga = HGRN_SLAB
    bt = q_ref.shape[0]
    n_chunks = bt // c
    slab = ga * c
    nb3 = slab // hb

    @pl.when(pl.program_id(1) == 0)
    def _():
        st_ref[...] = jnp.zeros_like(st_ref)

    lbp = lbp_ref[...]
    e = jnp.exp(lbp - jnp.max(lbp, axis=0, keepdims=True))
    sm = e / jnp.sum(e, axis=0, keepdims=True)
    cum = sm[0:1, :]
    for l in range(1, layer + 1):
        cum = cum + sm[l:l + 1, :]
    lower = jnp.clip(cum - sm[0:1, :], 0.0, MAX_LB)
    qscale = HGRN_DK ** -0.5
    row_in_chunk = lax.broadcasted_iota(jnp.int32, (slab, LANES), 0) % c
    blk_id = lax.broadcasted_iota(jnp.int32, (nb3, hb, LANES), 0) % nblk
    col = lax.broadcasted_iota(jnp.int32, (nb3, hb, LANES), 2) - hb * blk_id
    tr = lax.broadcasted_iota(jnp.int32, (ga, c, c), 1)
    ts = lax.broadcasted_iota(jnp.int32, (ga, c, c), 2)
    causal = tr >= ts

    def phase_a(si, carry):
        r0 = pl.multiple_of(si * slab, slab)
        rows = pl.ds(r0, slab)
        q = q_ref[rows, :].astype(f32) * qscale
        z = z_ref[rows, :].astype(f32)
        f = lower + (1.0 - lower) * jax.nn.sigmoid(z)
        g = jnp.log(jnp.maximum(f, MIN_FORGET))
        k = 1.0 - f
        b = _cumsum_chunks(g, row_in_chunk) * LOG2_E
        bc = b.reshape(ga, c, LANES)
        b_last = bc[:, c - 1:c, :]
        b_last_r = jnp.broadcast_to(b_last, (ga, c, LANES)).reshape(slab, LANES)
        vb = i_ref[rows, :]

        qd_ref[rows, :] = (q * jnp.exp2(b)).astype(bf16)
        kd = (k * jnp.exp2(b_last_r - b)).astype(bf16)

        q3 = q.reshape(nb3, hb, LANES)
        b3 = b.reshape(nb3, hb, LANES)
        bk_ref[0] = b3
        bk_ref[1] = k.reshape(nb3, hb, LANES)
        a_diag = jnp.zeros((nb3, hb, LANES), f32)
        for s in range(hb):
            w = jnp.exp2(b3 - bk_ref[0, :, s:s + 1, :])
            a = jnp.sum(q3 * bk_ref[1, :, s:s + 1, :] * w, axis=-1, keepdims=True)
            a_diag = jnp.where(col == s, a, a_diag)
        scores = a_diag.reshape(ga, c, LANES)[:, :, 0:c]

        group = 2 * hb
        while group <= c:
            n_groups = slab // group
            half = group // 2
            b_mid = b.reshape(n_groups, group, LANES)[:, half - 1:half, :]
            b_mid_r = jnp.broadcast_to(b_mid, (n_groups, group, LANES)).reshape(slab, LANES)
            upper = (row_in_chunk % group) >= half
            qt = jnp.where(upper, q * jnp.exp2(b - b_mid_r), 0.0).astype(bf16).reshape(ga, c, LANES)
            kt = jnp.where(upper, 0.0, k * jnp.exp2(b_mid_r - b)).astype(bf16).reshape(ga, c, LANES)
            a_off = jnp.einsum("gtk,gsk->gts", qt, kt, preferred_element_type=f32)
            if group < c:
                a_off = jnp.where((tr // group) == (ts // group), a_off, 0.0)
            scores = scores + a_off
            group *= 2
        sc_ref[rows, :] = jnp.where(causal, scores, 0.0).astype(bf16).reshape(slab, c)

        for gi in range(ga):
            upd_ref[si * ga + gi] = lax.dot_general(
                vb[gi * c:(gi + 1) * c, :], kd[gi * c:(gi + 1) * c, :],
                (((0,), (0,)), ((), ())), preferred_element_type=f32)
        dec_ref[pl.ds(si * ga, ga), :, :] = jnp.exp2(b_last)
        return carry

    lax.fori_loop(0, bt // slab, phase_a, 0)

    def phase_b(ci, st):
        stb_ref[ci] = st.astype(bf16)
        return st * dec_ref[ci] + upd_ref[ci]

    st_ref[...] = lax.fori_loop(0, n_chunks, phase_b, st_ref[...], unroll=4)

    ng = ng_ref[...]
    gc = HGRN_OUT_SLAB
    oslab = gc * c

    def phase_c(si, carry):
        r0 = pl.multiple_of(si * oslab, oslab)
        rows = pl.ds(r0, oslab)
        outs = []
        for gi in range(gc):
            cr = pl.ds(r0 + gi * c, c)
            inter = lax.dot_general(qd_ref[cr, :], stb_ref[si * gc + gi],
                                    (((1,), (1,)), ((), ())), preferred_element_type=f32)
            outs.append(inter + jnp.dot(sc_ref[cr, :], i_ref[cr, :], preferred_element_type=f32))
        o = jnp.concatenate(outs, axis=0)
        o = o * lax.rsqrt(jnp.mean(o * o, axis=-1, keepdims=True) + NORM_EPS) * ng
        og = og_ref[rows, :].astype(f32)
        o_ref[rows, :] = (o * (og * jax.nn.sigmoid(og))).astype(o_ref.dtype)
        return carry

    lax.fori_loop(0, bt // oslab, phase_c, 0)


def _hgrn(proj, lb_param, norm_g, layer, bt):
    s = proj.shape[0]
    depth = lb_param.shape[0]
    n_chunks = bt // HGRN_CHUNK
    cq, cf, ci, cg = (COL_HQ // LANES, COL_HF // LANES, COL_HI // LANES, COL_HG // LANES)
    return pl.pallas_call(
        functools.partial(_hgrn_body, layer=layer),
        grid=(HGRN_HEADS, s // bt),
        in_specs=[
            pl.BlockSpec((bt, LANES), lambda h, t: (t, cq + h)),
            pl.BlockSpec((bt, LANES), lambda h, t: (t, cf + h)),
            pl.BlockSpec((bt, LANES), lambda h, t: (t, ci + h)),
            pl.BlockSpec((bt, LANES), lambda h, t: (t, cg + h)),
            pl.BlockSpec((depth, LANES), lambda h, t: (0, h)),
            pl.BlockSpec((1, LANES), lambda h, t: (0, h)),
        ],
        out_specs=pl.BlockSpec((bt, LANES), lambda h, t: (t, h)),
        out_shape=jax.ShapeDtypeStruct((s, HGRN_IW), bf16),
        scratch_shapes=[
            pltpu.VMEM((HGRN_DV, HGRN_DK), f32),
            pltpu.VMEM((bt, LANES), bf16),
            pltpu.VMEM((n_chunks, HGRN_DV, HGRN_DK), bf16),
            pltpu.VMEM((n_chunks, HGRN_DV, HGRN_DK), f32),
            pltpu.VMEM((n_chunks, 1, LANES), f32),
            pltpu.VMEM((bt, HGRN_CHUNK), bf16),
            pltpu.VMEM((2, HGRN_SLAB * HGRN_CHUNK // HGRN_BLOCK, HGRN_BLOCK, LANES), f32),
        ],
        compiler_params=_params(("parallel", "arbitrary")),
        name="hgrn2",
    )(proj, proj, proj, proj, lb_param.astype(f32), norm_g.reshape(1, -1).astype(f32))


def _merge_body(a_ref, b_ref, pa_ref, ph_ref, ga_ref, gb_ref, o_ref):
    ya = jnp.dot(a_ref[...], pa_ref[...].astype(bf16), preferred_element_type=f32)
    yb = jnp.dot(b_ref[...], ph_ref[...].astype(bf16), preferred_element_type=f32)
    ga = jax.nn.sigmoid(ga_ref[...].astype(f32))
    gb = jax.nn.sigmoid(gb_ref[...].astype(f32))
    o_ref[...] = (ga * ya + gb * yb).astype(o_ref.dtype)


def _merge(a, b, pa, ph, proj, layer, bm, bn):
    m, ka = a.shape
    kb = b.shape[1]
    d = pa.shape[2]
    ca = COL_GA // bn
    cb = (COL_GA + d) // bn
    return pl.pallas_call(
        _merge_body,
        grid=(m // bm, d // bn),
        in_specs=[
            pl.BlockSpec((bm, ka), lambda i, j: (i, 0)),
            pl.BlockSpec((bm, kb), lambda i, j: (i, 0)),
            pl.BlockSpec((None, ka, bn), lambda i, j: (layer, 0, j)),
            pl.BlockSpec((None, kb, bn), lambda i, j: (layer, 0, j)),
            pl.BlockSpec((bm, bn), lambda i, j: (i, ca + j)),
            pl.BlockSpec((bm, bn), lambda i, j: (i, cb + j)),
        ],
        out_specs=pl.BlockSpec((bm, bn), lambda i, j: (i, j)),
        out_shape=jax.ShapeDtypeStruct((m, d), bf16),
        compiler_params=_params(("parallel", "arbitrary")),
        name="gated_merge",
    )(a, b, pa, ph, proj, proj)


def _resid_body(a_ref, w_ref, x_ref, o_ref):
    o_ref[...] = x_ref[...] + jnp.dot(a_ref[...], w_ref[...].astype(bf16), preferred_element_type=f32)


def _resid_matmul(a, w, x, layer, bm, bn, in_place):
    m, k = a.shape
    d = w.shape[2]
    return pl.pallas_call(
        _resid_body,
        grid=(m // bm, d // bn),
        in_specs=[
            pl.BlockSpec((bm, k), lambda i, j: (i, 0)),
            pl.BlockSpec((None, k, bn), lambda i, j: (layer, 0, j)),
            pl.BlockSpec((bm, bn), lambda i, j: (i, j)),
        ],
        out_specs=pl.BlockSpec((bm, bn), lambda i, j: (i, j)),
        out_shape=jax.ShapeDtypeStruct((m, d), f32),
        input_output_aliases={2: 0} if in_place else {},
        compiler_params=_params(("parallel", "arbitrary")),
        name="resid_matmul",
    )(a, w, x)


def _ffn_in_body(x_ref, xh_ref, g_ref, wu_ref, wg_ref, cw_ref, cb_ref, o_ref, h_ref, u_ref):
    bm = x_ref.shape[0]
    halo = BF16_ROWS

    @pl.when(pl.program_id(1) == 0)
    def _():
        _norm_rows_into(x_ref, g_ref, h_ref, bm, halo)

        @pl.when(pl.program_id(0) == 0)
        def _():
            h_ref[0:halo, :] = jnp.zeros((halo, h_ref.shape[1]), h_ref.dtype)

        @pl.when(pl.program_id(0) > 0)
        def _():
            _norm_rows_into(xh_ref, g_ref, h_ref, halo, 0)

    u_ref[...] = jnp.dot(h_ref[...], wu_ref[...].astype(bf16), preferred_element_type=f32)
    gate = jnp.dot(h_ref[halo:, :], wg_ref[...].astype(bf16), preferred_element_type=f32)
    cw = cw_ref[...]
    u = cb_ref[...] + cw[CONV_WIDTH - 1:CONV_WIDTH, :] * u_ref[halo:, :]
    for j in range(CONV_WIDTH - 1):
        back = CONV_WIDTH - 1 - j
        u = u + cw[j:j + 1, :] * u_ref[pl.ds(halo - back, bm), :]
    gelu = 0.5 * u * (1.0 + lax.erf(u * (2.0 ** -0.5)))
    o_ref[...] = (gelu * gate).astype(o_ref.dtype)


def _ffn_in(x, g, w, conv_w, conv_b, layer, bm, bn):
    m, d = x.shape
    ff = conv_w.shape[2]
    halo = BF16_ROWS
    nj = ff // bn
    return pl.pallas_call(
        _ffn_in_body,
        grid=(m // bm, nj),
        in_specs=[
            pl.BlockSpec((bm, d), lambda i, j: (i, 0), pipeline_mode=pl.Buffered(1)),
            pl.BlockSpec((halo, d), lambda i, j: (jnp.maximum(i * (bm // halo) - 1, 0), 0)),
            pl.BlockSpec((1, d), lambda i, j: (0, 0)),
            pl.BlockSpec((None, d, bn), lambda i, j: (layer, 0, j)),
            pl.BlockSpec((None, d, bn), lambda i, j: (layer, 0, nj + j)),
            pl.BlockSpec((None, CONV_WIDTH, bn), lambda i, j: (layer, 0, j)),
            pl.BlockSpec((1, bn), lambda i, j: (0, j)),
        ],
        out_specs=pl.BlockSpec((bm, bn), lambda i, j: (i, j)),
        out_shape=jax.ShapeDtypeStruct((m, ff), bf16),
        scratch_shapes=[pltpu.VMEM((bm + halo, d), bf16), pltpu.VMEM((bm + halo, bn), f32)],
        compiler_params=_params(("parallel", "arbitrary")),
        name="ffn_in",
    )(x, x, g.reshape(1, d), w, w, conv_w, conv_b.reshape(1, ff))


def _final_norm_body(x_ref, g_ref, o_ref):
    x = x_ref[...]
    ms = jnp.mean(x * x, axis=-1, keepdims=True)
    o_ref[...] = x * lax.rsqrt(ms + NORM_EPS) * g_ref[...]


def _final_norm(x, g, bm):
    m, d = x.shape
    return pl.pallas_call(
        _final_norm_body,
        grid=(m // bm,),
        in_specs=[pl.BlockSpec((bm, d), lambda i: (i, 0)), pl.BlockSpec((1, d), lambda i: (0, 0))],
        out_specs=pl.BlockSpec((bm, d), lambda i: (i, 0)),
        out_shape=jax.ShapeDtypeStruct((m, d), f32),
        compiler_params=_params(("parallel",)),
        name="final_norm",
    )(x, g.reshape(1, d))


def _tiles(seq):
    bm = min(1024, seq)
    return dict(bm=bm, bn=512, bn_ffn=256, bn_deep=256, bt=min(1024, seq),
                rope_rows=min(512, seq), norm_rows=min(256, seq))


def kernel(x, positions, norm1_g, w_in, attn_sinks, lb_param, hgrn_norm_g, p_attn, p_hgrn, w_out,
           norm2_g, w_ffn_in, conv_w, conv_b, w_down, final_norm_g):
    batch, seq, d = x.shape
    depth = w_in.shape[0]
    t = _tiles(seq)
    bm, bn = t["bm"], t["bn"]
    w_down = w_down.astype(bf16)
    outs = []
    for bi in range(batch):
        xs = x[bi]
        cos, sin = _rope_tables(positions[bi], t["rope_rows"])
        for l in range(depth):
            proj = _norm_proj(xs, norm1_g[l], w_in, cos, sin, l, bm, bn)
            a_out = _attention(proj, attn_sinks[l])
            b_out = _hgrn(proj, lb_param, hgrn_norm_g[l], l, t["bt"])
            merged = _merge(a_out, b_out, p_attn, p_hgrn, proj, l, bm, bn)
            xs = _resid_matmul(merged, w_out, xs, l, bm, bn, in_place=l > 0)
            act = _ffn_in(xs, norm2_g[l], w_ffn_in, conv_w, conv_b[l], l, bm, t["bn_ffn"])
            xs = _resid_matmul(act, w_down, xs, l, bm, t["bn_deep"], in_place=True)
        outs.append(_final_norm(xs, final_norm_g, t["norm_rows"]))
    return jnp.stack(outs, axis=0)
```

```python
import functools

import jax
import jax.numpy as jnp
from jax import lax
from jax.experimental import pallas as pl
from jax.experimental.pallas import tpu as pltpu

N_Q_HEADS = 32
N_KV_HEADS = 4
HEAD_DIM = 64
Q_GROUP = N_Q_HEADS // N_KV_HEADS
WINDOW = 128
ROPE_THETA = 10000.0
ATTN_W = N_Q_HEADS * HEAD_DIM
KV_W = N_KV_HEADS * HEAD_DIM
HGRN_HEADS = 16
HGRN_DK = 128
HGRN_DV = 128
HGRN_FW = HGRN_HEADS * HGRN_DK
HGRN_IW = HGRN_HEADS * HGRN_DV
CONV_WIDTH = 3
NORM_EPS = 1e-5
MASK_VALUE = -1e30
MIN_FORGET = 1e-30
MAX_LB = 0.999
LOG2_E = 1.4426950408889634

COL_Q = 0
COL_K = COL_Q + ATTN_W
COL_V = COL_K + KV_W
COL_HQ = COL_V + KV_W
COL_HF = COL_HQ + HGRN_FW
COL_HI = COL_HF + HGRN_FW
COL_HG = COL_HI + HGRN_IW
COL_GA = COL_HG + HGRN_IW

LANES = 128
BF16_ROWS = 16
VMEM_LIMIT = 56 * 1024 * 1024
HGRN_CHUNK = 64
HGRN_BLOCK = 8
HGRN_SLAB = 8
HGRN_OUT_SLAB = 16
NORM_ROWS = 64

f32 = jnp.float32
bf16 = jnp.bfloat16


def _params(sem):
    return pltpu.CompilerParams(dimension_semantics=sem, vmem_limit_bytes=VMEM_LIMIT)


def _norm_rows_into(x_ref, g_ref, h_ref, n_rows, dst_off):
    step = min(NORM_ROWS, n_rows)
    d = x_ref.shape[1]

    def body(c, carry):
        r0 = pl.multiple_of(c * step, step)
        rows = pl.ds(r0, step)
        ss = jnp.zeros((step, LANES), f32)
        for lc in range(d // LANES):
            xc = x_ref[rows, lc * LANES:(lc + 1) * LANES]
            ss = ss + xc * xc
        scale = lax.rsqrt(jnp.sum(ss, axis=-1, keepdims=True) * (1.0 / d) + NORM_EPS)
        for lc in range(d // LANES):
            lanes = slice(lc * LANES, (lc + 1) * LANES)
            y = x_ref[rows, lanes] * scale * g_ref[:, lanes]
            h_ref[pl.ds(dst_off + r0, step), lanes] = y.astype(h_ref.dtype)
        return carry

    lax.fori_loop(0, n_rows // step, body, 0)


def _proj_body(x_ref, g_ref, w_ref, cos_ref, sin_ref, o_ref, h_ref):
    j = pl.program_id(1)
    bn = o_ref.shape[1]

    @pl.when(j == 0)
    def _():
        _norm_rows_into(x_ref, g_ref, h_ref, x_ref.shape[0], 0)

    def matmul():
        w = w_ref[...].astype(bf16)
        return jnp.dot(h_ref[...], w, preferred_element_type=f32)

    q_blocks = COL_K // bn
    k_groups = (COL_V - COL_K) // LANES
    half = HEAD_DIM // 2

    def rope_store(acc, groups, cos, sin):
        first_half = (lax.broadcasted_iota(jnp.int32, cos.shape, 1) % HEAD_DIM) < half
        for c in range(bn // LANES):
            lanes = slice(c * LANES, (c + 1) * LANES)
            t = acc[:, lanes]
            if c < groups:
                fwd = pltpu.roll(t, LANES - half, axis=1)
                bwd = pltpu.roll(t, half, axis=1)
                t = t * cos + jnp.where(first_half, fwd, bwd) * sin
            o_ref[:, lanes] = t.astype(o_ref.dtype)

    @pl.when(j < q_blocks)
    def _():
        scale = HEAD_DIM ** -0.5
        rope_store(matmul(), bn // LANES, cos_ref[...] * scale, sin_ref[...] * scale)

    @pl.when(j == q_blocks)
    def _():
        rope_store(matmul(), k_groups, cos_ref[...], sin_ref[...])

    @pl.when(j > q_blocks)
    def _():
        o_ref[...] = matmul().astype(o_ref.dtype)


def _norm_proj(x, g, w, cos, sin, layer, bm, bn):
    m, d = x.shape
    n = w.shape[2]
    assert COL_K % bn == 0 and KV_W % LANES == 0 and KV_W <= bn
    return pl.pallas_call(
        _proj_body,
        grid=(m // bm, n // bn),
        in_specs=[
            pl.BlockSpec((bm, d), lambda i, j: (i, 0), pipeline_mode=pl.Buffered(1)),
            pl.BlockSpec((1, d), lambda i, j: (0, 0)),
            pl.BlockSpec((None, d, bn), lambda i, j: (layer, 0, j)),
            pl.BlockSpec((bm, LANES), lambda i, j: (i, 0)),
            pl.BlockSpec((bm, LANES), lambda i, j: (i, 0)),
        ],
        out_specs=pl.BlockSpec((bm, bn), lambda i, j: (i, j)),
        out_shape=jax.ShapeDtypeStruct((m, n), bf16),
        scratch_shapes=[pltpu.VMEM((bm, d), bf16)],
        compiler_params=_params(("parallel", "arbitrary")),
        name="norm_proj",
    )(x, g.reshape(1, d), w, cos, sin)


def _rope_body(pos_ref, inv_ref, cos_ref, sin_ref):
    ang = pos_ref[...].astype(f32) * inv_ref[...]
    lane = lax.broadcasted_iota(jnp.int32, ang.shape, 1)
    first_half = (lane % HEAD_DIM) < (HEAD_DIM // 2)
    cos_ref[...] = jnp.cos(ang)
    s = jnp.sin(ang)
    sin_ref[...] = jnp.where(first_half, -s, s)


def _rope_tables(positions, bs):
    s = positions.shape[-1]
    half = HEAD_DIM // 2
    inv_freq = ROPE_THETA ** (-jnp.arange(0, HEAD_DIM, 2, dtype=f32) / HEAD_DIM)
    inv_row = jnp.tile(inv_freq, LANES // half).reshape(1, LANES)
    return pl.pallas_call(
        _rope_body,
        grid=(s // bs,),
        in_specs=[pl.BlockSpec((bs, 1), lambda i: (i, 0)),
                  pl.BlockSpec((1, LANES), lambda i: (0, 0))],
        out_specs=[pl.BlockSpec((bs, LANES), lambda i: (i, 0))] * 2,
        out_shape=[jax.ShapeDtypeStruct((s, LANES), f32)] * 2,
        compiler_params=_params(("parallel",)),
        name="rope_tables",
    )(positions.reshape(s, 1), inv_row)


def _dup_head(pair, odd, low_lanes):
    swapped = pltpu.roll(pair, HEAD_DIM, axis=1)
    take_own = low_lanes != odd
    return jnp.where(take_own, pair, swapped)


def _attn_body(sink_ref, q_ref, kc_ref, kp_ref, vc_ref, vp_ref, o_ref, qs_ref, lg_ref, p_ref):
    n = pl.program_id(0)
    w = WINDOW
    pairs = Q_GROUP // 2
    low = lax.broadcasted_iota(jnp.int32, (w, LANES), 1) < HEAD_DIM
    low2 = lax.broadcasted_iota(jnp.int32, (2 * w, LANES), 1) < HEAD_DIM
    key_row = lax.broadcasted_iota(jnp.int32, (2 * w, LANES), 0)

    qi = lax.broadcasted_iota(jnp.int32, (w, 2 * w), 0)
    sj = lax.broadcasted_iota(jnp.int32, (w, 2 * w), 1)
    dist = qi + w - sj
    mask = (dist >= 0) & (dist < w) & ((n > 0) | (sj >= w))
    sink_col = lax.broadcasted_iota(jnp.int32, (1, 2 * w), 1) == 0

    for kg in range(N_KV_HEADS):
        c = kg // 2
        odd = (kg % 2) == 1
        cols = slice(c * LANES, (c + 1) * LANES)
        kband = jnp.concatenate([kp_ref[:, cols], kc_ref[:, cols]], axis=0).astype(f32)
        kk = _dup_head(kband, odd, low2).astype(bf16)
        vband = jnp.concatenate([vp_ref[:, cols], vc_ref[:, cols]], axis=0).astype(f32)
        vv = jnp.where(key_row == 0, 0.0, _dup_head(vband, odd, low2))
        ones = jnp.ones((2 * w, LANES), f32)
        v_even = jnp.concatenate([jnp.where(low2, vv, 0.0), jnp.where(low2, ones, 0.0)], axis=1).astype(bf16)
        v_odd = jnp.concatenate([jnp.where(low2, 0.0, vv), jnp.where(low2, 0.0, ones)], axis=1).astype(bf16)

        for p in range(pairs):
            col = kg * Q_GROUP * HEAD_DIM + p * LANES
            qp = q_ref[:, col:col + LANES]
            zero = jnp.zeros_like(qp)
            qs_ref[p * w:(p + 1) * w, :] = jnp.where(low, qp, zero)
            qs_ref[(pairs + p) * w:(pairs + p + 1) * w, :] = jnp.where(low, zero, qp)

        lg_ref[...] = lax.dot_general(qs_ref[...], kk, (((1,), (1,)), ((), ())),
                                      preferred_element_type=f32)
        for e in range(2):
            for p in range(pairs):
                slab = e * pairs + p
                rows = slice(slab * w, (slab + 1) * w)
                fill = jnp.where(sink_col, sink_ref[kg * Q_GROUP + 2 * p + e], MASK_VALUE)
                lg = jnp.where(mask, lg_ref[rows, :], fill)
                m = jnp.max(lg, axis=-1, keepdims=True)
                p_ref[rows, :] = jnp.exp(lg - m).astype(bf16)
        half_rows = pairs * w
        r = (jnp.dot(p_ref[0:half_rows, :], v_even, preferred_element_type=f32)
             + jnp.dot(p_ref[half_rows:, :], v_odd, preferred_element_type=f32))
        o = (r[:, 0:LANES] / r[:, LANES:]).astype(o_ref.dtype)
        for p in range(pairs):
            col = kg * Q_GROUP * HEAD_DIM + p * LANES
            o_ref[:, col:col + LANES] = o[p * w:(p + 1) * w, :]


def _attention(proj, sinks):
    s = proj.shape[0]
    w = WINDOW
    nb = s // w
    kcol, vcol = COL_K // KV_W, COL_V // KV_W
    grid_spec = pltpu.PrefetchScalarGridSpec(
        num_scalar_prefetch=1,
        grid=(nb,),
        in_specs=[
            pl.BlockSpec((w, ATTN_W), lambda n, sk: (n, 0)),
            pl.BlockSpec((w, KV_W), lambda n, sk: (n, kcol)),
            pl.BlockSpec((w, KV_W), lambda n, sk: (jnp.maximum(n - 1, 0), kcol)),
            pl.BlockSpec((w, KV_W), lambda n, sk: (n, vcol)),
            pl.BlockSpec((w, KV_W), lambda n, sk: (jnp.maximum(n - 1, 0), vcol)),
        ],
        out_specs=pl.BlockSpec((w, ATTN_W), lambda n, sk: (n, 0)),
        scratch_shapes=[
            pltpu.VMEM((Q_GROUP * w, LANES), bf16),
            pltpu.VMEM((Q_GROUP * w, 2 * w), f32),
            pltpu.VMEM((Q_GROUP * w, 2 * w), bf16),
        ],
    )
    return pl.pallas_call(
        _attn_body,
        grid_spec=grid_spec,
        out_shape=jax.ShapeDtypeStruct((s, ATTN_W), bf16),
        compiler_params=_params(("parallel",)),
        name="swa_attention",
    )(sinks.astype(f32), proj, proj, proj, proj, proj)


def _cumsum_chunks(g, row_in_chunk):
    b = g
    shift = 1
    while shift < HGRN_CHUNK:
        b = b + jnp.where(row_in_chunk >= shift, pltpu.roll(b, shift, axis=0), 0.0)
        shift *= 2
    return b


def _hgrn_body(q_ref, z_ref, i_ref, og_ref, lbp_ref, ng_ref, o_ref,
               st_ref, qd_ref, stb_ref, upd_ref, dec_ref, sc_ref, bk_ref, *, layer):
    c = HGRN_CHUNK
    hb = HGRN_BLOCK
    nblk = c // hb
    ga = HGRN_SLAB
    bt = q_ref.shape[0]
    n_chunks = bt // c
    slab = ga * c
    nb3 = slab // hb

    @pl.when(pl.program_id(1) == 0)
    def _():
        st_ref[...] = jnp.zeros_like(st_ref)

    lbp = lbp_ref[...]
    e = jnp.exp(lbp - jnp.max(lbp, axis=0, keepdims=True))
    sm = e / jnp.sum(e, axis=0, keepdims=True)
    cum = sm[0:1, :]
    for l in range(1, layer + 1):
        cum = cum + sm[l:l + 1, :]
    lower = jnp.clip(cum - sm[0:1, :], 0.0, MAX_LB)
    qscale = HGRN_DK ** -0.5
    row_in_chunk = lax.broadcasted_iota(jnp.int32, (slab, LANES), 0) % c
    blk_id = lax.broadcasted_iota(jnp.int32, (nb3, hb, LANES), 0) % nblk
    col = lax.broadcasted_iota(jnp.int32, (nb3, hb, LANES), 2) - hb * blk_id
    tr = lax.broadcasted_iota(jnp.int32, (ga, c, c), 1)
    ts = lax.broadcasted_iota(jnp.int32, (ga, c, c), 2)
    causal = tr >= ts

    def phase_a(si, carry):
        r0 = pl.multiple_of(si * slab, slab)
        rows = pl.ds(r0, slab)
        q = q_ref[rows, :].astype(f32) * qscale
        z = z_ref[rows, :].astype(f32)
        f = lower + (1.0 - lower) * jax.nn.sigmoid(z)
        g = jnp.log(jnp.maximum(f, MIN_FORGET))
        k = 1.0 - f
        b = _cumsum_chunks(g, row_in_chunk) * LOG2_E
        bc = b.reshape(ga, c, LANES)
        b_last = bc[:, c - 1:c, :]
        b_last_r = jnp.broadcast_to(b_last, (ga, c, LANES)).reshape(slab, LANES)
        vb = i_ref[rows, :]

        qd_ref[rows, :] = (q * jnp.exp2(b)).astype(bf16)
        kd = (k * jnp.exp2(b_last_r - b)).astype(bf16)

        q3 = q.reshape(nb3, hb, LANES)
        b3 = b.reshape(nb3, hb, LANES)
        bk_ref[0] = b3
        bk_ref[1] = k.reshape(nb3, hb, LANES)
        a_diag = jnp.zeros((nb3, hb, LANES), f32)
        for s in range(hb):
            w = jnp.exp2(b3 - bk_ref[0, :, s:s + 1, :])
            a = jnp.sum(q3 * bk_ref[1, :, s:s + 1, :] * w, axis=-1, keepdims=True)
            a_diag = jnp.where(col == s, a, a_diag)
        scores = a_diag.reshape(ga, c, LANES)[:, :, 0:c]

        group = 2 * hb
        while group <= c:
            n_groups = slab // group
            half = group // 2
            b_mid = b.reshape(n_groups, group, LANES)[:, half - 1:half, :]
            b_mid_r = jnp.broadcast_to(b_mid, (n_groups, group, LANES)).reshape(slab, LANES)
            upper = (row_in_chunk % group) >= half
            qt = jnp.where(upper, q * jnp.exp2(b - b_mid_r), 0.0).astype(bf16).reshape(ga, c, LANES)
            kt = jnp.where(upper, 0.0, k * jnp.exp2(b_mid_r - b)).astype(bf16).reshape(ga, c, LANES)
            a_off = jnp.einsum("gtk,gsk->gts", qt, kt, preferred_element_type=f32)
            if group < c:
                a_off = jnp.where((tr // group) == (ts // group), a_off, 0.0)
            scores = scores + a_off
            group *= 2
        sc_ref[rows, :] = jnp.where(causal, scores, 0.0).astype(bf16).reshape(slab, c)

        for gi in range(ga):
            upd_ref[si * ga + gi] = lax.dot_general(
                vb[gi * c:(gi + 1) * c, :], kd[gi * c:(gi + 1) * c, :],
                (((0,), (0,)), ((), ())), preferred_element_type=f32)
        dec_ref[pl.ds(si * ga, ga), :, :] = jnp.exp2(b_last)
        return carry

    lax.fori_loop(0, bt // slab, phase_a, 0)

    def phase_b(ci, st):
        stb_ref[ci] = st.astype(bf16)
        return st * dec_ref[ci] + upd_ref[ci]

    st_ref[...] = lax.fori_loop(0, n_chunks, phase_b, st_ref[...], unroll=4)

    ng = ng_ref[...]
    gc = HGRN_OUT_SLAB
    oslab = gc * c

    def phase_c(si, carry):
        r0 = pl.multiple_of(si * oslab, oslab)
        rows = pl.ds(r0, oslab)
        outs = []
        for gi in range(gc):
            cr = pl.ds(r0 + gi * c, c)
            inter = lax.dot_general(qd_ref[cr, :], stb_ref[si * gc + gi],
                                    (((1,), (1,)), ((), ())), preferred_element_type=f32)
            outs.append(inter + jnp.dot(sc_ref[cr, :], i_ref[cr, :], preferred_element_type=f32))
        o = jnp.concatenate(outs, axis=0)
        o = o * lax.rsqrt(jnp.mean(o * o, axis=-1, keepdims=True) + NORM_EPS) * ng
        og = og_ref[rows, :].astype(f32)
        o_ref[rows, :] = (o * (og * jax.nn.sigmoid(og))).astype(o_ref.dtype)
        return carry

    lax.fori_loop(0, bt // oslab, phase_c, 0)


def _hgrn(proj, lb_param, norm_g, layer, bt):
    s = proj.shape[0]
    depth = lb_param.shape[0]
    n_chunks = bt // HGRN_CHUNK
    cq, cf, ci, cg = (COL_HQ // LANES, COL_HF // LANES, COL_HI // LANES, COL_HG // LANES)
    return pl.pallas_call(
        functools.partial(_hgrn_body, layer=layer),
        grid=(HGRN_HEADS, s // bt),
        in_specs=[
            pl.BlockSpec((bt, LANES), lambda h, t: (t, cq + h)),
            pl.BlockSpec((bt, LANES), lambda h, t: (t, cf + h)),
            pl.BlockSpec((bt, LANES), lambda h, t: (t, ci + h)),
            pl.BlockSpec((bt, LANES), lambda h, t: (t, cg + h)),
            pl.BlockSpec((depth, LANES), lambda h, t: (0, h)),
            pl.BlockSpec((1, LANES), lambda h, t: (0, h)),
        ],
        out_specs=pl.BlockSpec((bt, LANES), lambda h, t: (t, h)),
        out_shape=jax.ShapeDtypeStruct((s, HGRN_IW), bf16),
        scratch_shapes=[
            pltpu.VMEM((HGRN_DV, HGRN_DK), f32),
            pltpu.VMEM((bt, LANES), bf16),
            pltpu.VMEM((n_chunks, HGRN_DV, HGRN_DK), bf16),
            pltpu.VMEM((n_chunks, HGRN_DV, HGRN_DK), f32),
            pltpu.VMEM((n_chunks, 1, LANES), f32),
            pltpu.VMEM((bt, HGRN_CHUNK), bf16),
            pltpu.VMEM((2, HGRN_SLAB * HGRN_CHUNK // HGRN_BLOCK, HGRN_BLOCK, LANES), f32),
        ],
        compiler_params=_params(("parallel", "arbitrary")),
        name="hgrn2",
    )(proj, proj, proj, proj, lb_param.astype(f32), norm_g.reshape(1, -1).astype(f32))


def _merge_body(a_ref, b_ref, pa_ref, ph_ref, ga_ref, gb_ref, o_ref):
    ya = jnp.dot(a_ref[...], pa_ref[...].astype(bf16), preferred_element_type=f32)
    yb = jnp.dot(b_ref[...], ph_ref[...].astype(bf16), preferred_element_type=f32)
    ga = jax.nn.sigmoid(ga_ref[...].astype(f32))
    gb = jax.nn.sigmoid(gb_ref[...].astype(f32))
    o_ref[...] = (ga * ya + gb * yb).astype(o_ref.dtype)


def _merge(a, b, pa, ph, proj, layer, bm, bn):
    m, ka = a.shape
    kb = b.shape[1]
    d = pa.shape[2]
    ca = COL_GA // bn
    cb = (COL_GA + d) // bn
    return pl.pallas_call(
        _merge_body,
        grid=(m // bm, d // bn),
        in_specs=[
            pl.BlockSpec((bm, ka), lambda i, j: (i, 0)),
            pl.BlockSpec((bm, kb), lambda i, j: (i, 0)),
            pl.BlockSpec((None, ka, bn), lambda i, j: (layer, 0, j)),
            pl.BlockSpec((None, kb, bn), lambda i, j: (layer, 0, j)),
            pl.BlockSpec((bm, bn), lambda i, j: (i, ca + j)),
            pl.BlockSpec((bm, bn), lambda i, j: (i, cb + j)),
        ],
        out_specs=pl.BlockSpec((bm, bn), lambda i, j: (i, j)),
        out_shape=jax.ShapeDtypeStruct((m, d), bf16),
        compiler_params=_params(("parallel", "arbitrary")),
        name="gated_merge",
    )(a, b, pa, ph, proj, proj)


def _resid_body(a_ref, w_ref, x_ref, o_ref):
    o_ref[...] = x_ref[...] + jnp.dot(a_ref[...], w_ref[...].astype(bf16), preferred_element_type=f32)


def _resid_matmul(a, w, x, layer, bm, bn, in_place):
    m, k = a.shape
    d = w.shape[2]
    return pl.pallas_call(
        _resid_body,
        grid=(m // bm, d // bn),
        in_specs=[
            pl.BlockSpec((bm, k), lambda i, j: (i, 0)),
            pl.BlockSpec((None, k, bn), lambda i, j: (layer, 0, j)),
            pl.BlockSpec((bm, bn), lambda i, j: (i, j)),
        ],
        out_specs=pl.BlockSpec((bm, bn), lambda i, j: (i, j)),
        out_shape=jax.ShapeDtypeStruct((m, d), f32),
        input_output_aliases={2: 0} if in_place else {},
        compiler_params=_params(("parallel", "arbitrary")),
        name="resid_matmul",
    )(a, w, x)


def _ffn_in_body(x_ref, xh_ref, g_ref, wu_ref, wg_ref, cw_ref, cb_ref, o_ref, h_ref, u_ref):
    bm = x_ref.shape[0]
    halo = BF16_ROWS

    @pl.when(pl.program_id(1) == 0)
    def _():
        _norm_rows_into(x_ref, g_ref, h_ref, bm, halo)

        @pl.when(pl.program_id(0) == 0)
        def _():
            h_ref[0:halo, :] = jnp.zeros((halo, h_ref.shape[1]), h_ref.dtype)

        @pl.when(pl.program_id(0) > 0)
        def _():
            _norm_rows_into(xh_ref, g_ref, h_ref, halo, 0)

    u_ref[...] = jnp.dot(h_ref[...], wu_ref[...].astype(bf16), preferred_element_type=f32)
    gate = jnp.dot(h_ref[halo:, :], wg_ref[...].astype(bf16), preferred_element_type=f32)
    cw = cw_ref[...]
    u = cb_ref[...] + cw[CONV_WIDTH - 1:CONV_WIDTH, :] * u_ref[halo:, :]
    for j in range(CONV_WIDTH - 1):
        back = CONV_WIDTH - 1 - j
        u = u + cw[j:j + 1, :] * u_ref[pl.ds(halo - back, bm), :]
    gelu = 0.5 * u * (1.0 + lax.erf(u * (2.0 ** -0.5)))
    o_ref[...] = (gelu * gate).astype(o_ref.dtype)


def _ffn_in(x, g, w, conv_w, conv_b, layer, bm, bn):
    m, d = x.shape
    ff = conv_w.shape[2]
    halo = BF16_ROWS
    nj = ff // bn
    return pl.pallas_call(
        _ffn_in_body,
        grid=(m // bm, nj),
        in_specs=[
            pl.BlockSpec((bm, d), lambda i, j: (i, 0), pipeline_mode=pl.Buffered(1)),
            pl.BlockSpec((halo, d), lambda i, j: (jnp.maximum(i * (bm // halo) - 1, 0), 0)),
            pl.BlockSpec((1, d), lambda i, j: (0, 0)),
            pl.BlockSpec((None, d, bn), lambda i, j: (layer, 0, j)),
            pl.BlockSpec((None, d, bn), lambda i, j: (layer, 0, nj + j)),
            pl.BlockSpec((None, CONV_WIDTH, bn), lambda i, j: (layer, 0, j)),
            pl.BlockSpec((1, bn), lambda i, j: (0, j)),
        ],
        out_specs=pl.BlockSpec((bm, bn), lambda i, j: (i, j)),
        out_shape=jax.ShapeDtypeStruct((m, ff), bf16),
        scratch_shapes=[pltpu.VMEM((bm + halo, d), bf16), pltpu.VMEM((bm + halo, bn), f32)],
        compiler_params=_params(("parallel", "arbitrary")),
        name="ffn_in",
    )(x, x, g.reshape(1, d), w, w, conv_w, conv_b.reshape(1, ff))


def _final_norm_body(x_ref, g_ref, o_ref):
    x = x_ref[...]
    ms = jnp.mean(x * x, axis=-1, keepdims=True)
    o_ref[...] = x * lax.rsqrt(ms + NORM_EPS) * g_ref[...]


def _final_norm(x, g, bm):
    m, d = x.shape
    return pl.pallas_call(
        _final_norm_body,
        grid=(m // bm,),
        in_specs=[pl.BlockSpec((bm, d), lambda i: (i, 0)), pl.BlockSpec((1, d), lambda i: (0, 0))],
        out_specs=pl.BlockSpec((bm, d), lambda i: (i, 0)),
        out_shape=jax.ShapeDtypeStruct((m, d), f32),
        compiler_params=_params(("parallel",)),
        name="final_norm",
    )(x, g.reshape(1, d))


def _tiles(seq):
    bm = min(1024, seq)
    return dict(bm=bm, bn=512, bn_ffn=256, bn_deep=256, bt=min(2048, seq),
                rope_rows=min(512, seq), norm_rows=min(256, seq))


def kernel(x, positions, norm1_g, w_in, attn_sinks, lb_param, hgrn_norm_g, p_attn, p_hgrn, w_out,
           norm2_g, w_ffn_in, conv_w, conv_b, w_down, final_norm_g):
    batch, seq, d = x.shape
    depth = w_in.shape[0]
    t = _tiles(seq)
    bm, bn = t["bm"], t["bn"]
    w_down = w_down.astype(bf16)
    outs = []
    for bi in range(batch):
        xs = x[bi]
        cos, sin = _rope_tables(positions[bi], t["rope_rows"])
        for l in range(depth):
            proj = _norm_proj(xs, norm1_g[l], w_in, cos, sin, l, bm, bn)
            a_out = _attention(proj, attn_sinks[l])
            b_out = _hgrn(proj, lb_param, hgrn_norm_g[l], l, t["bt"])
            merged = _merge(a_out, b_out, p_attn, p_hgrn, proj, l, bm, bn)
            xs = _resid_matmul(merged, w_out, xs, l, bm, bn, in_place=l > 0)
            act = _ffn_in(xs, norm2_g[l], w_ffn_in, conv_w, conv_b[l], l, bm, t["bn_ffn"])
            xs = _resid_matmul(act, w_down, xs, l, bm, t["bn_deep"], in_place=True)
        outs.append(_final_norm(xs, final_norm_g, t["norm_rows"]))
    return jnp.stack(outs, axis=0)
```

```python
import functools

import jax
import jax.numpy as jnp
from jax import lax
from jax.experimental import pallas as pl
from jax.experimental.pallas import tpu as pltpu

N_Q_HEADS = 32
N_KV_HEADS = 4
HEAD_DIM = 64
Q_GROUP = N_Q_HEADS // N_KV_HEADS
WINDOW = 128
ROPE_THETA = 10000.0
ATTN_W = N_Q_HEADS * HEAD_DIM
KV_W = N_KV_HEADS * HEAD_DIM
HGRN_HEADS = 16
HGRN_DK = 128
HGRN_DV = 128
HGRN_FW = HGRN_HEADS * HGRN_DK
HGRN_IW = HGRN_HEADS * HGRN_DV
CONV_WIDTH = 3
NORM_EPS = 1e-5
MASK_VALUE = -1e30
MIN_FORGET = 1e-30
MAX_LB = 0.999
LOG2_E = 1.4426950408889634

COL_Q = 0
COL_K = COL_Q + ATTN_W
COL_V = COL_K + KV_W
COL_HQ = COL_V + KV_W
COL_HF = COL_HQ + HGRN_FW
COL_HI = COL_HF + HGRN_FW
COL_HG = COL_HI + HGRN_IW
COL_GA = COL_HG + HGRN_IW

LANES = 128
BF16_ROWS = 16
VMEM_LIMIT = 56 * 1024 * 1024
HGRN_CHUNK = 64
HGRN_BLOCK = 8
HGRN_SLAB = 8
HGRN_OUT_SLAB = 16
NORM_ROWS = 64

f32 = jnp.float32
bf16 = jnp.bfloat16


def _params(sem):
    return pltpu.CompilerParams(dimension_semantics=sem, vmem_limit_bytes=VMEM_LIMIT)


def _norm_rows_into(x_ref, g_ref, h_ref, n_rows, dst_off):
    step = min(NORM_ROWS, n_rows)
    d = x_ref.shape[1]

    def body(c, carry):
        r0 = pl.multiple_of(c * step, step)
        rows = pl.ds(r0, step)
        ss = jnp.zeros((step, LANES), f32)
        for lc in range(d // LANES):
            xc = x_ref[rows, lc * LANES:(lc + 1) * LANES]
            ss = ss + xc * xc
        scale = lax.rsqrt(jnp.sum(ss, axis=-1, keepdims=True) * (1.0 / d) + NORM_EPS)
        for lc in range(d // LANES):
            lanes = slice(lc * LANES, (lc + 1) * LANES)
            y = x_ref[rows, lanes] * scale * g_ref[:, lanes]
            h_ref[pl.ds(dst_off + r0, step), lanes] = y.astype(h_ref.dtype)
        return carry

    lax.fori_loop(0, n_rows // step, body, 0)


def _proj_body(x_ref, g_ref, w_ref, cos_ref, sin_ref, o_ref, h_ref):
    j = pl.program_id(1)
    bn = o_ref.shape[1]

    @pl.when(j == 0)
    def _():
        _norm_rows_into(x_ref, g_ref, h_ref, x_ref.shape[0], 0)

    def matmul():
        w = w_ref[...].astype(bf16)
        return jnp.dot(h_ref[...], w, preferred_element_type=f32)

    q_blocks = COL_K // bn
    k_groups = (COL_V - COL_K) // LANES
    half = HEAD_DIM // 2

    def rope_store(acc, groups, cos, sin):
        first_half = (lax.broadcasted_iota(jnp.int32, cos.shape, 1) % HEAD_DIM) < half
        for c in range(bn // LANES):
            lanes = slice(c * LANES, (c + 1) * LANES)
            t = acc[:, lanes]
            if c < groups:
                fwd = pltpu.roll(t, LANES - half, axis=1)
                bwd = pltpu.roll(t, half, axis=1)
                t = t * cos + jnp.where(first_half, fwd, bwd) * sin
            o_ref[:, lanes] = t.astype(o_ref.dtype)

    @pl.when(j < q_blocks)
    def _():
        scale = HEAD_DIM ** -0.5
        rope_store(matmul(), bn // LANES, cos_ref[...] * scale, sin_ref[...] * scale)

    @pl.when(j == q_blocks)
    def _():
        rope_store(matmul(), k_groups, cos_ref[...], sin_ref[...])

    @pl.when(j > q_blocks)
    def _():
        o_ref[...] = matmul().astype(o_ref.dtype)


def _norm_proj(x, g, w, cos, sin, layer, bm, bn):
    m, d = x.shape
    n = w.shape[2]
    assert COL_K % bn == 0 and KV_W % LANES == 0 and KV_W <= bn
    return pl.pallas_call(
        _proj_body,
        grid=(m // bm, n // bn),
        in_specs=[
            pl.BlockSpec((bm, d), lambda i, j: (i, 0), pipeline_mode=pl.Buffered(1)),
            pl.BlockSpec((1, d), lambda i, j: (0, 0)),
            pl.BlockSpec((None, d, bn), lambda i, j: (layer, 0, j)),
            pl.BlockSpec((bm, LANES), lambda i, j: (i, 0)),
            pl.BlockSpec((bm, LANES), lambda i, j: (i, 0)),
        ],
        out_specs=pl.BlockSpec((bm, bn), lambda i, j: (i, j)),
        out_shape=jax.ShapeDtypeStruct((m, n), bf16),
        scratch_shapes=[pltpu.VMEM((bm, d), bf16)],
        compiler_params=_params(("parallel", "arbitrary")),
        name="norm_proj",
    )(x, g.reshape(1, d), w, cos, sin)


def _rope_body(pos_ref, inv_ref, cos_ref, sin_ref):
    ang = pos_ref[...].astype(f32) * inv_ref[...]
    lane = lax.broadcasted_iota(jnp.int32, ang.shape, 1)
    first_half = (lane % HEAD_DIM) < (HEAD_DIM // 2)
    cos_ref[...] = jnp.cos(ang)
    s = jnp.sin(ang)
    sin_ref[...] = jnp.where(first_half, -s, s)


def _rope_tables(positions, bs):
    s = positions.shape[-1]
    half = HEAD_DIM // 2
    inv_freq = ROPE_THETA ** (-jnp.arange(0, HEAD_DIM, 2, dtype=f32) / HEAD_DIM)
    inv_row = jnp.tile(inv_freq, LANES // half).reshape(1, LANES)
    return pl.pallas_call(
        _rope_body,
        grid=(s // bs,),
        in_specs=[pl.BlockSpec((bs, 1), lambda i: (i, 0)),
                  pl.BlockSpec((1, LANES), lambda i: (0, 0))],
        out_specs=[pl.BlockSpec((bs, LANES), lambda i: (i, 0))] * 2,
        out_shape=[jax.ShapeDtypeStruct((s, LANES), f32)] * 2,
        compiler_params=_params(("parallel",)),
        name="rope_tables",
    )(positions.reshape(s, 1), inv_row)


def _dup_head(pair, odd, low_lanes):
    swapped = pltpu.roll(pair, HEAD_DIM, axis=1)
    take_own = low_lanes != odd
    return jnp.where(take_own, pair, swapped)


def _attn_body(sink_ref, q_ref, kc_ref, kp_ref, vc_ref, vp_ref, o_ref, qs_ref, lg_ref, p_ref):
    n = pl.program_id(0)
    w = WINDOW
    pairs = Q_GROUP // 2
    low = lax.broadcasted_iota(jnp.int32, (w, LANES), 1) < HEAD_DIM
    low2 = lax.broadcasted_iota(jnp.int32, (2 * w, LANES), 1) < HEAD_DIM
    key_row = lax.broadcasted_iota(jnp.int32, (2 * w, LANES), 0)

    qi = lax.broadcasted_iota(jnp.int32, (w, 2 * w), 0)
    sj = lax.broadcasted_iota(jnp.int32, (w, 2 * w), 1)
    dist = qi + w - sj
    mask = (dist >= 0) & (dist < w) & ((n > 0) | (sj >= w))
    sink_col = lax.broadcasted_iota(jnp.int32, (1, 2 * w), 1) == 0

    for kg in range(N_KV_HEADS):
        c = kg // 2
        odd = (kg % 2) == 1
        cols = slice(c * LANES, (c + 1) * LANES)
        kband = jnp.concatenate([kp_ref[:, cols], kc_ref[:, cols]], axis=0).astype(f32)
        kk = _dup_head(kband, odd, low2).astype(bf16)
        vband = jnp.concatenate([vp_ref[:, cols], vc_ref[:, cols]], axis=0).astype(f32)
        vv = jnp.where(key_row == 0, 0.0, _dup_head(vband, odd, low2))
        ones = jnp.ones((2 * w, LANES), f32)
        v_even = jnp.concatenate([jnp.where(low2, vv, 0.0), jnp.where(low2, ones, 0.0)], axis=1).astype(bf16)
        v_odd = jnp.concatenate([jnp.where(low2, 0.0, vv), jnp.where(low2, 0.0, ones)], axis=1).astype(bf16)

        for p in range(pairs):
            col = kg * Q_GROUP * HEAD_DIM + p * LANES
            qp = q_ref[:, col:col + LANES]
            zero = jnp.zeros_like(qp)
            qs_ref[p * w:(p + 1) * w, :] = jnp.where(low, qp, zero)
            qs_ref[(pairs + p) * w:(pairs + p + 1) * w, :] = jnp.where(low, zero, qp)

        lg_ref[...] = lax.dot_general(qs_ref[...], kk, (((1,), (1,)), ((), ())),
                                      preferred_element_type=f32)
        for e in range(2):
            for p in range(pairs):
                slab = e * pairs + p
                rows = slice(slab * w, (slab + 1) * w)
                fill = jnp.where(sink_col, sink_ref[kg * Q_GROUP + 2 * p + e], MASK_VALUE)
                lg = jnp.where(mask, lg_ref[rows, :], fill)
                m = jnp.max(lg, axis=-1, keepdims=True)
                p_ref[rows, :] = jnp.exp(lg - m).astype(bf16)
        half_rows = pairs * w
        r = (jnp.dot(p_ref[0:half_rows, :], v_even, preferred_element_type=f32)
             + jnp.dot(p_ref[half_rows:, :], v_odd, preferred_element_type=f32))
        o = (r[:, 0:LANES] / r[:, LANES:]).astype(o_ref.dtype)
        for p in range(pairs):
            col = kg * Q_GROUP * HEAD_DIM + p * LANES
            o_ref[:, col:col + LANES] = o[p * w:(p + 1) * w, :]


def _attention(proj, sinks):
    s = proj.shape[0]
    w = WINDOW
    nb = s // w
    kcol, vcol = COL_K // KV_W, COL_V // KV_W
    grid_spec = pltpu.PrefetchScalarGridSpec(
        num_scalar_prefetch=1,
        grid=(nb,),
        in_specs=[
            pl.BlockSpec((w, ATTN_W), lambda n, sk: (n, 0)),
            pl.BlockSpec((w, KV_W), lambda n, sk: (n, kcol)),
            pl.BlockSpec((w, KV_W), lambda n, sk: (jnp.maximum(n - 1, 0), kcol)),
            pl.BlockSpec((w, KV_W), lambda n, sk: (n, vcol)),
            pl.BlockSpec((w, KV_W), lambda n, sk: (jnp.maximum(n - 1, 0), vcol)),
        ],
        out_specs=pl.BlockSpec((w, ATTN_W), lambda n, sk: (n, 0)),
        scratch_shapes=[
            pltpu.VMEM((Q_GROUP * w, LANES), bf16),
            pltpu.VMEM((Q_GROUP * w, 2 * w), f32),
            pltpu.VMEM((Q_GROUP * w, 2 * w), bf16),
        ],
    )
    return pl.pallas_call(
        _attn_body,
        grid_spec=grid_spec,
        out_shape=jax.ShapeDtypeStruct((s, ATTN_W), bf16),
        compiler_params=_params(("parallel",)),
        name="swa_attention",
    )(sinks.astype(f32), proj, proj, proj, proj, proj)


def _cumsum_chunks(g, row_in_chunk):
    b = g
    shift = 1
    while shift < HGRN_CHUNK:
        b = b + jnp.where(row_in_chunk >= shift, pltpu.roll(b, shift, axis=0), 0.0)
        shift *= 2
    return b


def _hgrn_body(q_ref, z_ref, i_ref, og_ref, lbp_ref, ng_ref, o_ref,
               st_ref, qd_ref, stb_ref, upd_ref, dec_ref, sc_ref, bk_ref, *, layer):
    c = HGRN_CHUNK
    hb = HGRN_BLOCK
    nblk = c // hb
    ga = HGRN_SLAB
    bt = q_ref.shape[0]
    n_chunks = bt // c
    slab = ga * c
    nb3 = slab // hb

    @pl.when(pl.program_id(1) == 0)
    def _():
        st_ref[...] = jnp.zeros_like(st_ref)

    lbp = lbp_ref[...]
    e = jnp.exp(lbp - jnp.max(lbp, axis=0, keepdims=True))
    sm = e / jnp.sum(e, axis=0, keepdims=True)
    cum = sm[0:1, :]
    for l in range(1, layer + 1):
        cum = cum + sm[l:l + 1, :]
    lower = jnp.clip(cum - sm[0:1, :], 0.0, MAX_LB)
    qscale = HGRN_DK ** -0.5
    row_in_chunk = lax.broadcasted_iota(jnp.int32, (slab, LANES), 0) % c
    blk_id = lax.broadcasted_iota(jnp.int32, (nb3, hb, LANES), 0) % nblk
    col = lax.broadcasted_iota(jnp.int32, (nb3, hb, LANES), 2) - hb * blk_id
    tr = lax.broadcasted_iota(jnp.int32, (ga, c, c), 1)
    ts = lax.broadcasted_iota(jnp.int32, (ga, c, c), 2)
    causal = tr >= ts

    def phase_a(si, carry):
        r0 = pl.multiple_of(si * slab, slab)
        rows = pl.ds(r0, slab)
        q = q_ref[rows, :].astype(f32) * qscale
        z = z_ref[rows, :].astype(f32)
        f = lower + (1.0 - lower) * jax.nn.sigmoid(z)
        g = jnp.log(jnp.maximum(f, MIN_FORGET))
        k = 1.0 - f
        b = _cumsum_chunks(g, row_in_chunk) * LOG2_E
        bc = b.reshape(ga, c, LANES)
        b_last = bc[:, c - 1:c, :]
        b_last_r = jnp.broadcast_to(b_last, (ga, c, LANES)).reshape(slab, LANES)
        vb = i_ref[rows, :]

        qd_ref[rows, :] = (q * jnp.exp2(b)).astype(bf16)
        kd = (k * jnp.exp2(b_last_r - b)).astype(bf16)

        q3 = q.reshape(nb3, hb, LANES)
        b3 = b.reshape(nb3, hb, LANES)
        bk_ref[0] = b3
        bk_ref[1] = k.reshape(nb3, hb, LANES)
        a_diag = jnp.zeros((nb3, hb, LANES), f32)
        for s in range(hb):
            w = jnp.exp2(b3 - bk_ref[0, :, s:s + 1, :])
            a = jnp.sum(q3 * bk_ref[1, :, s:s + 1, :] * w, axis=-1, keepdims=True)
            a_diag = jnp.where(col == s, a, a_diag)
        scores = a_diag.reshape(ga, c, LANES)[:, :, 0:c]

        group = 2 * hb
        while group <= c:
            n_groups = slab // group
            half = group // 2
            b_mid = b.reshape(n_groups, group, LANES)[:, half - 1:half, :]
            b_mid_r = jnp.broadcast_to(b_mid, (n_groups, group, LANES)).reshape(slab, LANES)
            upper = (row_in_chunk % group) >= half
            qt = jnp.where(upper, q * jnp.exp2(b - b_mid_r), 0.0).astype(bf16).reshape(ga, c, LANES)
            kt = jnp.where(upper, 0.0, k * jnp.exp2(b_mid_r - b)).astype(bf16).reshape(ga, c, LANES)
            a_off = jnp.einsum("gtk,gsk->gts", qt, kt, preferred_element_type=f32)
            if group < c:
                a_off = jnp.where((tr // group) == (ts // group), a_off, 0.0)
            scores = scores + a_off
            group *= 2
        sc_ref[rows, :] = jnp.where(causal, scores, 0.0).astype(bf16).reshape(slab, c)

        for gi in range(ga):
            upd_ref[si * ga + gi] = lax.dot_general(
                vb[gi * c:(gi + 1) * c, :], kd[gi * c:(gi + 1) * c, :],
                (((0,), (0,)), ((), ())), preferred_element_type=f32)
        dec_ref[pl.ds(si * ga, ga), :, :] = jnp.exp2(b_last)
        return carry

    lax.fori_loop(0, bt // slab, phase_a, 0)

    def phase_b(ci, st):
        stb_ref[ci] = st.astype(bf16)
        return st * dec_ref[ci] + upd_ref[ci]

    st_ref[...] = lax.fori_loop(0, n_chunks, phase_b, st_ref[...], unroll=4)

    ng = ng_ref[...]
    gc = HGRN_OUT_SLAB
    oslab = gc * c

    def phase_c(si, carry):
        r0 = pl.multiple_of(si * oslab, oslab)
        rows = pl.ds(r0, oslab)
        outs = []
        for gi in range(gc):
            cr = pl.ds(r0 + gi * c, c)
            inter = lax.dot_general(qd_ref[cr, :], stb_ref[si * gc + gi],
                                    (((1,), (1,)), ((), ())), preferred_element_type=f32)
            outs.append(inter + jnp.dot(sc_ref[cr, :], i_ref[cr, :], preferred_element_type=f32))
        o = jnp.concatenate(outs, axis=0)
        o = o * lax.rsqrt(jnp.mean(o * o, axis=-1, keepdims=True) + NORM_EPS) * ng
        og = og_ref[rows, :].astype(f32)
        o_ref[rows, :] = (o * (og * jax.nn.sigmoid(og))).astype(o_ref.dtype)
        return carry

    lax.fori_loop(0, bt // oslab, phase_c, 0)


def _hgrn(proj, lb_param, norm_g, layer, bt):
    s = proj.shape[0]
    depth = lb_param.shape[0]
    n_chunks = bt // HGRN_CHUNK
    cq, cf, ci, cg = (COL_HQ // LANES, COL_HF // LANES, COL_HI // LANES, COL_HG // LANES)
    return pl.pallas_call(
        functools.partial(_hgrn_body, layer=layer),
        grid=(HGRN_HEADS, s // bt),
        in_specs=[
            pl.BlockSpec((bt, LANES), lambda h, t: (t, cq + h)),
            pl.BlockSpec((bt, LANES), lambda h, t: (t, cf + h)),
            pl.BlockSpec((bt, LANES), lambda h, t: (t, ci + h)),
            pl.BlockSpec((bt, LANES), lambda h, t: (t, cg + h)),
            pl.BlockSpec((depth, LANES), lambda h, t: (0, h)),
            pl.BlockSpec((1, LANES), lambda h, t: (0, h)),
        ],
        out_specs=pl.BlockSpec((bt, LANES), lambda h, t: (t, h)),
        out_shape=jax.ShapeDtypeStruct((s, HGRN_IW), bf16),
        scratch_shapes=[
            pltpu.VMEM((HGRN_DV, HGRN_DK), f32),
            pltpu.VMEM((bt, LANES), bf16),
            pltpu.VMEM((n_chunks, HGRN_DV, HGRN_DK), bf16),
            pltpu.VMEM((n_chunks, HGRN_DV, HGRN_DK), f32),
            pltpu.VMEM((n_chunks, 1, LANES), f32),
            pltpu.VMEM((bt, HGRN_CHUNK), bf16),
            pltpu.VMEM((2, HGRN_SLAB * HGRN_CHUNK // HGRN_BLOCK, HGRN_BLOCK, LANES), f32),
        ],
        compiler_params=_params(("parallel", "arbitrary")),
        name="hgrn2",
    )(proj, proj, proj, proj, lb_param.astype(f32), norm_g.reshape(1, -1).astype(f32))


def _merge_body(a_ref, b_ref, pa_ref, ph_ref, ga_ref, gb_ref, o_ref):
    ya = jnp.dot(a_ref[...], pa_ref[...].astype(bf16), preferred_element_type=f32)
    yb = jnp.dot(b_ref[...], ph_ref[...].astype(bf16), preferred_element_type=f32)
    ga = jax.nn.sigmoid(ga_ref[...].astype(f32))
    gb = jax.nn.sigmoid(gb_ref[...].astype(f32))
    o_ref[...] = (ga * ya + gb * yb).astype(o_ref.dtype)


def _merge(a, b, pa, ph, proj, layer, bm, bn):
    m, ka = a.shape
    kb = b.shape[1]
    d = pa.shape[2]
    ca = COL_GA // bn
    cb = (COL_GA + d) // bn
    return pl.pallas_call(
        _merge_body,
        grid=(m // bm, d // bn),
        in_specs=[
            pl.BlockSpec((bm, ka), lambda i, j: (i, 0)),
            pl.BlockSpec((bm, kb), lambda i, j: (i, 0)),
            pl.BlockSpec((None, ka, bn), lambda i, j: (layer, 0, j)),
            pl.BlockSpec((None, kb, bn), lambda i, j: (layer, 0, j)),
            pl.BlockSpec((bm, bn), lambda i, j: (i, ca + j)),
            pl.BlockSpec((bm, bn), lambda i, j: (i, cb + j)),
        ],
        out_specs=pl.BlockSpec((bm, bn), lambda i, j: (i, j)),
        out_shape=jax.ShapeDtypeStruct((m, d), bf16),
        compiler_params=_params(("parallel", "arbitrary")),
        name="gated_merge",
    )(a, b, pa, ph, proj, proj)


def _resid_body(a_ref, w_ref, x_ref, o_ref):
    o_ref[...] = x_ref[...] + jnp.dot(a_ref[...], w_ref[...].astype(bf16), preferred_element_type=f32)


def _resid_matmul(a, w, x, layer, bm, bn, in_place):
    m, k = a.shape
    d = w.shape[2]
    return pl.pallas_call(
        _resid_body,
        grid=(m // bm, d // bn),
        in_specs=[
            pl.BlockSpec((bm, k), lambda i, j: (i, 0)),
            pl.BlockSpec((None, k, bn), lambda i, j: (layer, 0, j)),
            pl.BlockSpec((bm, bn), lambda i, j: (i, j)),
        ],
        out_specs=pl.BlockSpec((bm, bn), lambda i, j: (i, j)),
        out_shape=jax.ShapeDtypeStruct((m, d), f32),
        input_output_aliases={2: 0} if in_place else {},
        compiler_params=_params(("parallel", "arbitrary")),
        name="resid_matmul",
    )(a, w, x)


def _ffn_in_body(x_ref, xh_ref, g_ref, wu_ref, wg_ref, cw_ref, cb_ref, o_ref, h_ref, u_ref):
    bm = x_ref.shape[0]
    halo = BF16_ROWS

    @pl.when(pl.program_id(1) == 0)
    def _():
        _norm_rows_into(x_ref, g_ref, h_ref, bm, halo)

        @pl.when(pl.program_id(0) == 0)
        def _():
            h_ref[0:halo, :] = jnp.zeros((halo, h_ref.shape[1]), h_ref.dtype)

        @pl.when(pl.program_id(0) > 0)
        def _():
            _norm_rows_into(xh_ref, g_ref, h_ref, halo, 0)

    u_ref[...] = jnp.dot(h_ref[...], wu_ref[...].astype(bf16), preferred_element_type=f32)
    gate = jnp.dot(h_ref[halo:, :], wg_ref[...].astype(bf16), preferred_element_type=f32)
    cw = cw_ref[...]
    u = cb_ref[...] + cw[CONV_WIDTH - 1:CONV_WIDTH, :] * u_ref[halo:, :]
    for j in range(CONV_WIDTH - 1):
        back = CONV_WIDTH - 1 - j
        u = u + cw[j:j + 1, :] * u_ref[pl.ds(halo - back, bm), :]
    gelu = 0.5 * u * (1.0 + lax.erf(u * (2.0 ** -0.5)))
    o_ref[...] = (gelu * gate).astype(o_ref.dtype)


def _ffn_in(x, g, w, conv_w, conv_b, layer, bm, bn):
    m, d = x.shape
    ff = conv_w.shape[2]
    halo = BF16_ROWS
    nj = ff // bn
    return pl.pallas_call(
        _ffn_in_body,
        grid=(m // bm, nj),
        in_specs=[
            pl.BlockSpec((bm, d), lambda i, j: (i, 0), pipeline_mode=pl.Buffered(1)),
            pl.BlockSpec((halo, d), lambda i, j: (jnp.maximum(i * (bm // halo) - 1, 0), 0)),
            pl.BlockSpec((1, d), lambda i, j: (0, 0)),
            pl.BlockSpec((None, d, bn), lambda i, j: (layer, 0, j)),
            pl.BlockSpec((None, d, bn), lambda i, j: (layer, 0, nj + j)),
            pl.BlockSpec((None, CONV_WIDTH, bn), lambda i, j: (layer, 0, j)),
            pl.BlockSpec((1, bn), lambda i, j: (0, j)),
        ],
        out_specs=pl.BlockSpec((bm, bn), lambda i, j: (i, j)),
        out_shape=jax.ShapeDtypeStruct((m, ff), bf16),
        scratch_shapes=[pltpu.VMEM((bm + halo, d), bf16), pltpu.VMEM((bm + halo, bn), f32)],
        compiler_params=_params(("parallel", "arbitrary")),
        name="ffn_in",
    )(x, x, g.reshape(1, d), w, w, conv_w, conv_b.reshape(1, ff))


def _final_norm_body(x_ref, g_ref, o_ref):
    x = x_ref[...]
    ms = jnp.mean(x * x, axis=-1, keepdims=True)
    o_ref[...] = x * lax.rsqrt(ms + NORM_EPS) * g_ref[...]


def _final_norm(x, g, bm):
    m, d = x.shape
    return pl.pallas_call(
        _final_norm_body,
        grid=(m // bm,),
        in_specs=[pl.BlockSpec((bm, d), lambda i: (i, 0)), pl.BlockSpec((1, d), lambda i: (0, 0))],
        out_specs=pl.BlockSpec((bm, d), lambda i: (i, 0)),
        out_shape=jax.ShapeDtypeStruct((m, d), f32),
        compiler_params=_params(("parallel",)),
        name="final_norm",
    )(x, g.reshape(1, d))


def _tiles(seq):
    bm = min(1024, seq)
    return dict(bm=bm, bn=512, bn_ffn=256, bn_deep=256, bt=min(8192, seq),
                rope_rows=min(512, seq), norm_rows=min(256, seq))


def kernel(x, positions, norm1_g, w_in, attn_sinks, lb_param, hgrn_norm_g, p_attn, p_hgrn, w_out,
           norm2_g, w_ffn_in, conv_w, conv_b, w_down, final_norm_g):
    batch, seq, d = x.shape
    depth = w_in.shape[0]
    t = _tiles(seq)
    bm, bn = t["bm"], t["bn"]
    w_down = w_down.astype(bf16)
    outs = []
    for bi in range(batch):
        xs = x[bi]
        cos, sin = _rope_tables(positions[bi], t["rope_rows"])
        for l in range(depth):
            proj = _norm_proj(xs, norm1_g[l], w_in, cos, sin, l, bm, bn)
            a_out = _attention(proj, attn_sinks[l])
            b_out = _hgrn(proj, lb_param, hgrn_norm_g[l], l, t["bt"])
            merged = _merge(a_out, b_out, p_attn, p_hgrn, proj, l, bm, bn)
            xs = _resid_matmul(merged, w_out, xs, l, bm, bn, in_place=l > 0)
            act = _ffn_in(xs, norm2_g[l], w_ffn_in, conv_w, conv_b[l], l, bm, t["bn_ffn"])
            xs = _resid_matmul(act, w_down, xs, l, bm, t["bn_deep"], in_place=True)
        outs.append(_final_norm(xs, final_norm_g, t["norm_rows"]))
    return jnp.stack(outs, axis=0)
```

```python
import functools

import jax
import jax.numpy as jnp
from jax import lax
from jax.experimental import pallas as pl
from jax.experimental.pallas import tpu as pltpu

N_Q_HEADS = 32
N_KV_HEADS = 4
HEAD_DIM = 64
Q_GROUP = N_Q_HEADS // N_KV_HEADS
WINDOW = 128
ROPE_THETA = 10000.0
ATTN_W = N_Q_HEADS * HEAD_DIM
KV_W = N_KV_HEADS * HEAD_DIM
HGRN_HEADS = 16
HGRN_DK = 128
HGRN_DV = 128
HGRN_FW = HGRN_HEADS * HGRN_DK
HGRN_IW = HGRN_HEADS * HGRN_DV
CONV_WIDTH = 3
NORM_EPS = 1e-5
MASK_VALUE = -1e30
MIN_FORGET = 1e-30
MAX_LB = 0.999
LOG2_E = 1.4426950408889634

COL_Q = 0
COL_K = COL_Q + ATTN_W
COL_V = COL_K + KV_W
COL_HQ = COL_V + KV_W
COL_HF = COL_HQ + HGRN_FW
COL_HI = COL_HF + HGRN_FW
COL_HG = COL_HI + HGRN_IW
COL_GA = COL_HG + HGRN_IW

LANES = 128
BF16_ROWS = 16
VMEM_LIMIT = 56 * 1024 * 1024
HGRN_CHUNK = 64
HGRN_BLOCK = 8
HGRN_SLAB = 8
HGRN_OUT_SLAB = 16
NORM_ROWS = 64

f32 = jnp.float32
bf16 = jnp.bfloat16


def _params(sem):
    return pltpu.CompilerParams(dimension_semantics=sem, vmem_limit_bytes=VMEM_LIMIT)


def _norm_rows_into(x_ref, g_ref, h_ref, n_rows, dst_off):
    step = min(NORM_ROWS, n_rows)
    d = x_ref.shape[1]

    def body(c, carry):
        r0 = pl.multiple_of(c * step, step)
        rows = pl.ds(r0, step)
        ss = jnp.zeros((step, LANES), f32)
        for lc in range(d // LANES):
            xc = x_ref[rows, lc * LANES:(lc + 1) * LANES]
            ss = ss + xc * xc
        scale = lax.rsqrt(jnp.sum(ss, axis=-1, keepdims=True) * (1.0 / d) + NORM_EPS)
        for lc in range(d // LANES):
            lanes = slice(lc * LANES, (lc + 1) * LANES)
            y = x_ref[rows, lanes] * scale * g_ref[:, lanes]
            h_ref[pl.ds(dst_off + r0, step), lanes] = y.astype(h_ref.dtype)
        return carry

    lax.fori_loop(0, n_rows // step, body, 0)


def _proj_body(x_ref, g_ref, w_ref, cos_ref, sin_ref, o_ref, h_ref):
    j = pl.program_id(1)
    bn = o_ref.shape[1]

    @pl.when(j == 0)
    def _():
        _norm_rows_into(x_ref, g_ref, h_ref, x_ref.shape[0], 0)

    def matmul():
        w = w_ref[...].astype(bf16)
        return jnp.dot(h_ref[...], w, preferred_element_type=f32)

    q_blocks = COL_K // bn
    k_groups = (COL_V - COL_K) // LANES
    half = HEAD_DIM // 2

    def rope_store(acc, groups, cos, sin):
        first_half = (lax.broadcasted_iota(jnp.int32, cos.shape, 1) % HEAD_DIM) < half
        for c in range(bn // LANES):
            lanes = slice(c * LANES, (c + 1) * LANES)
            t = acc[:, lanes]
            if c < groups:
                fwd = pltpu.roll(t, LANES - half, axis=1)
                bwd = pltpu.roll(t, half, axis=1)
                t = t * cos + jnp.where(first_half, fwd, bwd) * sin
            o_ref[:, lanes] = t.astype(o_ref.dtype)

    @pl.when(j < q_blocks)
    def _():
        scale = HEAD_DIM ** -0.5
        rope_store(matmul(), bn // LANES, cos_ref[...] * scale, sin_ref[...] * scale)

    @pl.when(j == q_blocks)
    def _():
        rope_store(matmul(), k_groups, cos_ref[...], sin_ref[...])

    @pl.when(j > q_blocks)
    def _():
        o_ref[...] = matmul().astype(o_ref.dtype)


def _norm_proj(x, g, w, cos, sin, layer, bm, bn):
    m, d = x.shape
    n = w.shape[2]
    assert COL_K % bn == 0 and KV_W % LANES == 0 and KV_W <= bn
    return pl.pallas_call(
        _proj_body,
        grid=(m // bm, n // bn),
        in_specs=[
            pl.BlockSpec((bm, d), lambda i, j: (i, 0), pipeline_mode=pl.Buffered(1)),
            pl.BlockSpec((1, d), lambda i, j: (0, 0)),
            pl.BlockSpec((None, d, bn), lambda i, j: (layer, 0, j)),
            pl.BlockSpec((bm, LANES), lambda i, j: (i, 0)),
            pl.BlockSpec((bm, LANES), lambda i, j: (i, 0)),
        ],
        out_specs=pl.BlockSpec((bm, bn), lambda i, j: (i, j)),
        out_shape=jax.ShapeDtypeStruct((m, n), bf16),
        scratch_shapes=[pltpu.VMEM((bm, d), bf16)],
        compiler_params=_params(("parallel", "arbitrary")),
        name="norm_proj",
    )(x, g.reshape(1, d), w, cos, sin)


def _rope_body(pos_ref, inv_ref, cos_ref, sin_ref):
    ang = pos_ref[...].astype(f32) * inv_ref[...]
    lane = lax.broadcasted_iota(jnp.int32, ang.shape, 1)
    first_half = (lane % HEAD_DIM) < (HEAD_DIM // 2)
    cos_ref[...] = jnp.cos(ang)
    s = jnp.sin(ang)
    sin_ref[...] = jnp.where(first_half, -s, s)


def _rope_tables(positions, bs):
    s = positions.shape[-1]
    half = HEAD_DIM // 2
    inv_freq = ROPE_THETA ** (-jnp.arange(0, HEAD_DIM, 2, dtype=f32) / HEAD_DIM)
    inv_row = jnp.tile(inv_freq, LANES // half).reshape(1, LANES)
    return pl.pallas_call(
        _rope_body,
        grid=(s // bs,),
        in_specs=[pl.BlockSpec((bs, 1), lambda i: (i, 0)),
                  pl.BlockSpec((1, LANES), lambda i: (0, 0))],
        out_specs=[pl.BlockSpec((bs, LANES), lambda i: (i, 0))] * 2,
        out_shape=[jax.ShapeDtypeStruct((s, LANES), f32)] * 2,
        compiler_params=_params(("parallel",)),
        name="rope_tables",
    )(positions.reshape(s, 1), inv_row)


def _dup_head(pair, odd, low_lanes):
    swapped = pltpu.roll(pair, HEAD_DIM, axis=1)
    take_own = low_lanes != odd
    return jnp.where(take_own, pair, swapped)


def _attn_body(sink_ref, q_ref, kc_ref, kp_ref, vc_ref, vp_ref, o_ref, qs_ref, lg_ref, p_ref):
    n = pl.program_id(0)
    w = WINDOW
    pairs = Q_GROUP // 2
    low = lax.broadcasted_iota(jnp.int32, (w, LANES), 1) < HEAD_DIM
    low2 = lax.broadcasted_iota(jnp.int32, (2 * w, LANES), 1) < HEAD_DIM
    key_row = lax.broadcasted_iota(jnp.int32, (2 * w, LANES), 0)

    qi = lax.broadcasted_iota(jnp.int32, (w, 2 * w), 0)
    sj = lax.broadcasted_iota(jnp.int32, (w, 2 * w), 1)
    dist = qi + w - sj
    mask = (dist >= 0) & (dist < w) & ((n > 0) | (sj >= w))
    sink_col = lax.broadcasted_iota(jnp.int32, (1, 2 * w), 1) == 0

    for kg in range(N_KV_HEADS):
        c = kg // 2
        odd = (kg % 2) == 1
        cols = slice(c * LANES, (c + 1) * LANES)
        kband = jnp.concatenate([kp_ref[:, cols], kc_ref[:, cols]], axis=0).astype(f32)
        kk = _dup_head(kband, odd, low2).astype(bf16)
        vband = jnp.concatenate([vp_ref[:, cols], vc_ref[:, cols]], axis=0).astype(f32)
        vv = jnp.where(key_row == 0, 0.0, _dup_head(vband, odd, low2))
        ones = jnp.ones((2 * w, LANES), f32)
        v_even = jnp.concatenate([jnp.where(low2, vv, 0.0), jnp.where(low2, ones, 0.0)], axis=1).astype(bf16)
        v_odd = jnp.concatenate([jnp.where(low2, 0.0, vv), jnp.where(low2, 0.0, ones)], axis=1).astype(bf16)

        for p in range(pairs):
            col = kg * Q_GROUP * HEAD_DIM + p * LANES
            qp = q_ref[:, col:col + LANES]
            zero = jnp.zeros_like(qp)
            qs_ref[p * w:(p + 1) * w, :] = jnp.where(low, qp, zero)
            qs_ref[(pairs + p) * w:(pairs + p + 1) * w, :] = jnp.where(low, zero, qp)

        lg_ref[...] = lax.dot_general(qs_ref[...], kk, (((1,), (1,)), ((), ())),
                                      preferred_element_type=f32)
        for e in range(2):
            for p in range(pairs):
                slab = e * pairs + p
                rows = slice(slab * w, (slab + 1) * w)
                fill = jnp.where(sink_col, sink_ref[kg * Q_GROUP + 2 * p + e], MASK_VALUE)
                lg = jnp.where(mask, lg_ref[rows, :], fill)
                m = jnp.max(lg, axis=-1, keepdims=True)
                p_ref[rows, :] = jnp.exp(lg - m).astype(bf16)
        half_rows = pairs * w
        r = (jnp.dot(p_ref[0:half_rows, :], v_even, preferred_element_type=f32)
             + jnp.dot(p_ref[half_rows:, :], v_odd, preferred_element_type=f32))
        o = (r[:, 0:LANES] / r[:, LANES:]).astype(o_ref.dtype)
        for p in range(pairs):
            col = kg * Q_GROUP * HEAD_DIM + p * LANES
            o_ref[:, col:col + LANES] = o[p * w:(p + 1) * w, :]


def _attention(proj, sinks):
    s = proj.shape[0]
    w = WINDOW
    nb = s // w
    kcol, vcol = COL_K // KV_W, COL_V // KV_W
    grid_spec = pltpu.PrefetchScalarGridSpec(
        num_scalar_prefetch=1,
        grid=(nb,),
        in_specs=[
            pl.BlockSpec((w, ATTN_W), lambda n, sk: (n, 0)),
            pl.BlockSpec((w, KV_W), lambda n, sk: (n, kcol)),
            pl.BlockSpec((w, KV_W), lambda n, sk: (jnp.maximum(n - 1, 0), kcol)),
            pl.BlockSpec((w, KV_W), lambda n, sk: (n, vcol)),
            pl.BlockSpec((w, KV_W), lambda n, sk: (jnp.maximum(n - 1, 0), vcol)),
        ],
        out_specs=pl.BlockSpec((w, ATTN_W), lambda n, sk: (n, 0)),
        scratch_shapes=[
            pltpu.VMEM((Q_GROUP * w, LANES), bf16),
            pltpu.VMEM((Q_GROUP * w, 2 * w), f32),
            pltpu.VMEM((Q_GROUP * w, 2 * w), bf16),
        ],
    )
    return pl.pallas_call(
        _attn_body,
        grid_spec=grid_spec,
        out_shape=jax.ShapeDtypeStruct((s, ATTN_W), bf16),
        compiler_params=_params(("parallel",)),
        name="swa_attention",
    )(sinks.astype(f32), proj, proj, proj, proj, proj)


def _cumsum_chunks(g, row_in_chunk):
    b = g
    shift = 1
    while shift < HGRN_CHUNK:
        b = b + jnp.where(row_in_chunk >= shift, pltpu.roll(b, shift, axis=0), 0.0)
        shift *= 2
    return b


def _hgrn_body(q_ref, z_ref, i_ref, og_ref, lbp_ref, ng_ref, o_ref,
               st_ref, qd_ref, stb_ref, upd_ref, dec_ref, sc_ref, bk_ref, *, layer):
    c = HGRN_CHUNK
    hb = HGRN_BLOCK
    nblk = c // hb
    ga = HGRN_SLAB
    bt = q_ref.shape[0]
    n_chunks = bt // c
    slab = ga * c
    nb3 = slab // hb

    @pl.when(pl.program_id(1) == 0)
    def _():
        st_ref[...] = jnp.zeros_like(st_ref)

    lbp = lbp_ref[...]
    e = jnp.exp(lbp - jnp.max(lbp, axis=0, keepdims=True))
    sm = e / jnp.sum(e, axis=0, keepdims=True)
    cum = sm[0:1, :]
    for l in range(1, layer + 1):
        cum = cum + sm[l:l + 1, :]
    lower = jnp.clip(cum - sm[0:1, :], 0.0, MAX_LB)
    qscale = HGRN_DK ** -0.5
    row_in_chunk = lax.broadcasted_iota(jnp.int32, (slab, LANES), 0) % c
    blk_id = lax.broadcasted_iota(jnp.int32, (nb3, hb, LANES), 0) % nblk
    col = lax.broadcasted_iota(jnp.int32, (nb3, hb, LANES), 2) - hb * blk_id
    tr = lax.broadcasted_iota(jnp.int32, (ga, c, c), 1)
    ts = lax.broadcasted_iota(jnp.int32, (ga, c, c), 2)
    causal = tr >= ts

    def phase_a(si, carry):
        r0 = pl.multiple_of(si * slab, slab)
        rows = pl.ds(r0, slab)
        q = q_ref[rows, :].astype(f32) * qscale
        z = z_ref[rows, :].astype(f32)
        f = lower + (1.0 - lower) * jax.nn.sigmoid(z)
        g = jnp.log(jnp.maximum(f, MIN_FORGET))
        k = 1.0 - f
        b = _cumsum_chunks(g, row_in_chunk) * LOG2_E
        bc = b.reshape(ga, c, LANES)
        b_last = bc[:, c - 1:c, :]
        b_last_r = jnp.broadcast_to(b_last, (ga, c, LANES)).reshape(slab, LANES)
        vb = i_ref[rows, :]

        qd_ref[rows, :] = (q * jnp.exp2(b)).astype(bf16)
        kd = (k * jnp.exp2(b_last_r - b)).astype(bf16)

        q3 = q.reshape(nb3, hb, LANES)
        b3 = b.reshape(nb3, hb, LANES)
        bk_ref[0] = b3
        bk_ref[1] = k.reshape(nb3, hb, LANES)
        a_diag = jnp.zeros((nb3, hb, LANES), f32)
        for s in range(hb):
            w = jnp.exp2(b3 - bk_ref[0, :, s:s + 1, :])
            a = jnp.sum(q3 * bk_ref[1, :, s:s + 1, :] * w, axis=-1, keepdims=True)
            a_diag = jnp.where(col == s, a, a_diag)
        scores = a_diag.reshape(ga, c, LANES)[:, :, 0:c]

        group = 2 * hb
        while group <= c:
            n_groups = slab // group
            half = group // 2
            b_mid = b.reshape(n_groups, group, LANES)[:, half - 1:half, :]
            b_mid_r = jnp.broadcast_to(b_mid, (n_groups, group, LANES)).reshape(slab, LANES)
            upper = (row_in_chunk % group) >= half
            qt = jnp.where(upper, q * jnp.exp2(b - b_mid_r), 0.0).astype(bf16).reshape(ga, c, LANES)
            kt = jnp.where(upper, 0.0, k * jnp.exp2(b_mid_r - b)).astype(bf16).reshape(ga, c, LANES)
            a_off = jnp.einsum("gtk,gsk->gts", qt, kt, preferred_element_type=f32)
            if group < c:
                a_off = jnp.where((tr // group) == (ts // group), a_off, 0.0)
            scores = scores + a_off
            group *= 2
        sc_ref[rows, :] = jnp.where(causal, scores, 0.0).astype(bf16).reshape(slab, c)

        for gi in range(ga):
            upd_ref[si * ga + gi] = lax.dot_general(
                vb[gi * c:(gi + 1) * c, :], kd[gi * c:(gi + 1) * c, :],
                (((0,), (0,)), ((), ())), preferred_element_type=f32)
        dec_ref[pl.ds(si * ga, ga), :, :] = jnp.exp2(b_last)
        return carry

    lax.fori_loop(0, bt // slab, phase_a, 0)

    def phase_b(ci, st):
        stb_ref[ci] = st.astype(bf16)
        return st * dec_ref[ci] + upd_ref[ci]

    st_ref[...] = lax.fori_loop(0, n_chunks, phase_b, st_ref[...], unroll=4)

    ng = ng_ref[...]
    gc = HGRN_OUT_SLAB
    oslab = gc * c

    def phase_c(si, carry):
        r0 = pl.multiple_of(si * oslab, oslab)
        rows = pl.ds(r0, oslab)
        outs = []
        for gi in range(gc):
            cr = pl.ds(r0 + gi * c, c)
            inter = lax.dot_general(qd_ref[cr, :], stb_ref[si * gc + gi],
                                    (((1,), (1,)), ((), ())), preferred_element_type=f32)
            outs.append(inter + jnp.dot(sc_ref[cr, :], i_ref[cr, :], preferred_element_type=f32))
        o = jnp.concatenate(outs, axis=0)
        o = o * lax.rsqrt(jnp.mean(o * o, axis=-1, keepdims=True) + NORM_EPS) * ng
        og = og_ref[rows, :].astype(f32)
        o_ref[rows, :] = (o * (og * jax.nn.sigmoid(og))).astype(o_ref.dtype)
        return carry

    lax.fori_loop(0, bt // oslab, phase_c, 0)


def _hgrn(proj, lb_param, norm_g, layer, bt):
    s = proj.shape[0]
    depth = lb_param.shape[0]
    n_chunks = bt // HGRN_CHUNK
    cq, cf, ci, cg = (COL_HQ // LANES, COL_HF // LANES, COL_HI // LANES, COL_HG // LANES)
    return pl.pallas_call(
        functools.partial(_hgrn_body, layer=layer),
        grid=(HGRN_HEADS, s // bt),
        in_specs=[
            pl.BlockSpec((bt, LANES), lambda h, t: (t, cq + h)),
            pl.BlockSpec((bt, LANES), lambda h, t: (t, cf + h)),
            pl.BlockSpec((bt, LANES), lambda h, t: (t, ci + h)),
            pl.BlockSpec((bt, LANES), lambda h, t: (t, cg + h)),
            pl.BlockSpec((depth, LANES), lambda h, t: (0, h)),
            pl.BlockSpec((1, LANES), lambda h, t: (0, h)),
        ],
        out_specs=pl.BlockSpec((bt, LANES), lambda h, t: (t, h)),
        out_shape=jax.ShapeDtypeStruct((s, HGRN_IW), bf16),
        scratch_shapes=[
            pltpu.VMEM((HGRN_DV, HGRN_DK), f32),
            pltpu.VMEM((bt, LANES), bf16),
            pltpu.VMEM((n_chunks, HGRN_DV, HGRN_DK), bf16),
            pltpu.VMEM((n_chunks, HGRN_DV, HGRN_DK), f32),
            pltpu.VMEM((n_chunks, 1, LANES), f32),
            pltpu.VMEM((bt, HGRN_CHUNK), bf16),
            pltpu.VMEM((2, HGRN_SLAB * HGRN_CHUNK // HGRN_BLOCK, HGRN_BLOCK, LANES), f32),
        ],
        compiler_params=_params(("parallel", "arbitrary")),
        name="hgrn2",
    )(proj, proj, proj, proj, lb_param.astype(f32), norm_g.reshape(1, -1).astype(f32))


def _merge_body(a_ref, b_ref, pa_ref, ph_ref, ga_ref, gb_ref, o_ref):
    ya = jnp.dot(a_ref[...], pa_ref[...].astype(bf16), preferred_element_type=f32)
    yb = jnp.dot(b_ref[...], ph_ref[...].astype(bf16), preferred_element_type=f32)
    ga = jax.nn.sigmoid(ga_ref[...].astype(f32))
    gb = jax.nn.sigmoid(gb_ref[...].astype(f32))
    o_ref[...] = (ga * ya + gb * yb).astype(o_ref.dtype)


def _merge(a, b, pa, ph, proj, layer, bm, bn):
    m, ka = a.shape
    kb = b.shape[1]
    d = pa.shape[2]
    ca = COL_GA // bn
    cb = (COL_GA + d) // bn
    return pl.pallas_call(
        _merge_body,
        grid=(m // bm, d // bn),
        in_specs=[
            pl.BlockSpec((bm, ka), lambda i, j: (i, 0)),
            pl.BlockSpec((bm, kb), lambda i, j: (i, 0)),
            pl.BlockSpec((None, ka, bn), lambda i, j: (layer, 0, j)),
            pl.BlockSpec((None, kb, bn), lambda i, j: (layer, 0, j)),
            pl.BlockSpec((bm, bn), lambda i, j: (i, ca + j)),
            pl.BlockSpec((bm, bn), lambda i, j: (i, cb + j)),
        ],
        out_specs=pl.BlockSpec((bm, bn), lambda i, j: (i, j)),
        out_shape=jax.ShapeDtypeStruct((m, d), bf16),
        compiler_params=_params(("parallel", "arbitrary")),
        name="gated_merge",
    )(a, b, pa, ph, proj, proj)


def _resid_body(a_ref, w_ref, x_ref, o_ref):
    o_ref[...] = x_ref[...] + jnp.dot(a_ref[...], w_ref[...].astype(bf16), preferred_element_type=f32)


def _resid_norm_body(a_ref, w_ref, x_ref, g_ref, o_ref):
    j = pl.program_id(1)
    bm, bn = x_ref.shape
    d = o_ref.shape[1]
    cols = pl.ds(pl.multiple_of(j * bn, bn), bn)
    o_ref[:, cols] = x_ref[...] + jnp.dot(a_ref[...], w_ref[...].astype(bf16),
                                          preferred_element_type=f32)

    @pl.when(j == pl.num_programs(1) - 1)
    def _():
        step = min(NORM_ROWS, bm)

        def body(c, carry):
            rows = pl.ds(pl.multiple_of(c * step, step), step)
            ss = jnp.zeros((step, LANES), f32)
            for lc in range(d // LANES):
                xc = o_ref[rows, lc * LANES:(lc + 1) * LANES]
                ss = ss + xc * xc
            scale = lax.rsqrt(jnp.sum(ss, axis=-1, keepdims=True) * (1.0 / d) + NORM_EPS)
            for lc in range(d // LANES):
                lanes = slice(lc * LANES, (lc + 1) * LANES)
                o_ref[rows, lanes] = o_ref[rows, lanes] * scale * g_ref[:, lanes]
            return carry

        lax.fori_loop(0, bm // step, body, 0)


def _resid_matmul_norm(a, w, x, g, layer, bm, bn):
    m, k = a.shape
    d = w.shape[2]
    return pl.pallas_call(
        _resid_norm_body,
        grid=(m // bm, d // bn),
        in_specs=[
            pl.BlockSpec((bm, k), lambda i, j: (i, 0)),
            pl.BlockSpec((None, k, bn), lambda i, j: (layer, 0, j)),
            pl.BlockSpec((bm, bn), lambda i, j: (i, j)),
            pl.BlockSpec((1, d), lambda i, j: (0, 0)),
        ],
        out_specs=pl.BlockSpec((bm, d), lambda i, j: (i, 0)),
        out_shape=jax.ShapeDtypeStruct((m, d), f32),
        compiler_params=_params(("parallel", "arbitrary")),
        name="resid_matmul_norm",
    )(a, w, x, g.reshape(1, d))


def _resid_matmul(a, w, x, layer, bm, bn, in_place):
    m, k = a.shape
    d = w.shape[2]
    return pl.pallas_call(
        _resid_body,
        grid=(m // bm, d // bn),
        in_specs=[
            pl.BlockSpec((bm, k), lambda i, j: (i, 0)),
            pl.BlockSpec((None, k, bn), lambda i, j: (layer, 0, j)),
            pl.BlockSpec((bm, bn), lambda i, j: (i, j)),
        ],
        out_specs=pl.BlockSpec((bm, bn), lambda i, j: (i, j)),
        out_shape=jax.ShapeDtypeStruct((m, d), f32),
        input_output_aliases={2: 0} if in_place else {},
        compiler_params=_params(("parallel", "arbitrary")),
        name="resid_matmul",
    )(a, w, x)


def _ffn_in_body(x_ref, xh_ref, g_ref, wu_ref, wg_ref, cw_ref, cb_ref, o_ref, h_ref, u_ref):
    bm = x_ref.shape[0]
    halo = BF16_ROWS

    @pl.when(pl.program_id(1) == 0)
    def _():
        _norm_rows_into(x_ref, g_ref, h_ref, bm, halo)

        @pl.when(pl.program_id(0) == 0)
        def _():
            h_ref[0:halo, :] = jnp.zeros((halo, h_ref.shape[1]), h_ref.dtype)

        @pl.when(pl.program_id(0) > 0)
        def _():
            _norm_rows_into(xh_ref, g_ref, h_ref, halo, 0)

    u_ref[...] = jnp.dot(h_ref[...], wu_ref[...].astype(bf16), preferred_element_type=f32)
    gate = jnp.dot(h_ref[halo:, :], wg_ref[...].astype(bf16), preferred_element_type=f32)
    cw = cw_ref[...]
    u = cb_ref[...] + cw[CONV_WIDTH - 1:CONV_WIDTH, :] * u_ref[halo:, :]
    for j in range(CONV_WIDTH - 1):
        back = CONV_WIDTH - 1 - j
        u = u + cw[j:j + 1, :] * u_ref[pl.ds(halo - back, bm), :]
    gelu = 0.5 * u * (1.0 + lax.erf(u * (2.0 ** -0.5)))
    o_ref[...] = (gelu * gate).astype(o_ref.dtype)


def _ffn_in(x, g, w, conv_w, conv_b, layer, bm, bn):
    m, d = x.shape
    ff = conv_w.shape[2]
    halo = BF16_ROWS
    nj = ff // bn
    return pl.pallas_call(
        _ffn_in_body,
        grid=(m // bm, nj),
        in_specs=[
            pl.BlockSpec((bm, d), lambda i, j: (i, 0), pipeline_mode=pl.Buffered(1)),
            pl.BlockSpec((halo, d), lambda i, j: (jnp.maximum(i * (bm // halo) - 1, 0), 0)),
            pl.BlockSpec((1, d), lambda i, j: (0, 0)),
            pl.BlockSpec((None, d, bn), lambda i, j: (layer, 0, j)),
            pl.BlockSpec((None, d, bn), lambda i, j: (layer, 0, nj + j)),
            pl.BlockSpec((None, CONV_WIDTH, bn), lambda i, j: (layer, 0, j)),
            pl.BlockSpec((1, bn), lambda i, j: (0, j)),
        ],
        out_specs=pl.BlockSpec((bm, bn), lambda i, j: (i, j)),
        out_shape=jax.ShapeDtypeStruct((m, ff), bf16),
        scratch_shapes=[pltpu.VMEM((bm + halo, d), bf16), pltpu.VMEM((bm + halo, bn), f32)],
        compiler_params=_params(("parallel", "arbitrary")),
        name="ffn_in",
    )(x, x, g.reshape(1, d), w, w, conv_w, conv_b.reshape(1, ff))


def _final_norm_body(x_ref, g_ref, o_ref):
    x = x_ref[...]
    ms = jnp.mean(x * x, axis=-1, keepdims=True)
    o_ref[...] = x * lax.rsqrt(ms + NORM_EPS) * g_ref[...]


def _final_norm(x, g, bm):
    m, d = x.shape
    return pl.pallas_call(
        _final_norm_body,
        grid=(m // bm,),
        in_specs=[pl.BlockSpec((bm, d), lambda i: (i, 0)), pl.BlockSpec((1, d), lambda i: (0, 0))],
        out_specs=pl.BlockSpec((bm, d), lambda i: (i, 0)),
        out_shape=jax.ShapeDtypeStruct((m, d), f32),
        compiler_params=_params(("parallel",)),
        name="final_norm",
    )(x, g.reshape(1, d))


def _tiles(seq):
    bm = min(1024, seq)
    return dict(bm=bm, bn=512, bn_ffn=256, bn_deep=256, bt=min(2048, seq),
                rope_rows=min(512, seq), norm_rows=min(256, seq))


def kernel(x, positions, norm1_g, w_in, attn_sinks, lb_param, hgrn_norm_g, p_attn, p_hgrn, w_out,
           norm2_g, w_ffn_in, conv_w, conv_b, w_down, final_norm_g):
    batch, seq, d = x.shape
    depth = w_in.shape[0]
    t = _tiles(seq)
    bm, bn = t["bm"], t["bn"]
    w_down = w_down.astype(bf16)
    outs = []
    for bi in range(batch):
        xs = x[bi]
        cos, sin = _rope_tables(positions[bi], t["rope_rows"])
        for l in range(depth):
            proj = _norm_proj(xs, norm1_g[l], w_in, cos, sin, l, bm, bn)
            a_out = _attention(proj, attn_sinks[l])
            b_out = _hgrn(proj, lb_param, hgrn_norm_g[l], l, t["bt"])
            merged = _merge(a_out, b_out, p_attn, p_hgrn, proj, l, bm, bn)
            xs = _resid_matmul(merged, w_out, xs, l, bm, bn, in_place=l > 0)
            act = _ffn_in(xs, norm2_g[l], w_ffn_in, conv_w, conv_b[l], l, bm, t["bn_ffn"])
            if l < depth - 1:
                xs = _resid_matmul(act, w_down, xs, l, bm, t["bn_deep"], in_place=True)
            else:
                xs = _resid_matmul_norm(act, w_down, xs, final_norm_g, l, min(bm, 512), t["bn_deep"])
        outs.append(xs)
    return jnp.stack(outs, axis=0)
```
